```python
import math
import jax
import jax.numpy as jnp
from jax import lax
import numpy as np

D_MODEL = 2048
BATCH = 16
SEQ = 2048
DEPTH = 2

ATTN_WIDTH = D_MODEL // 2
ATTN_HEAD_DIM = 128
N_ATTN_HEADS = ATTN_WIDTH // ATTN_HEAD_DIM
MOBA_BLOCK = 256
MOBA_TOPK = 3
MOBA_Q_CHUNK = 16
ROPE_THETA = 10000.0

SSM_WIDTH = D_MODEL
SSM_HEAD_DIM = 64
N_SSM_HEADS = SSM_WIDTH // SSM_HEAD_DIM
SSM_GROUPS = 8
SSM_STATE = 128
SSM_CONV = 4
SSD_CHUNK = 256
DT_MIN = 1e-3
DT_MAX = 1e-1

MIX_WIDTH = ATTN_WIDTH + SSM_WIDTH
XBC_WIDTH = SSM_WIDTH + 2 * SSM_GROUPS * SSM_STATE
IN_SIZES = (ATTN_WIDTH, ATTN_WIDTH, ATTN_WIDTH, SSM_WIDTH, XBC_WIDTH, N_SSM_HEADS)
IN_WIDTH = sum(IN_SIZES)
IN_OFFSETS = tuple(int(o) for o in np.cumsum(IN_SIZES)[:-1])

D_FF = ((8 * D_MODEL // 3 + 255) // 256) * 256
FFN_CONV = 3

NORM_EPS = 1e-6
NEG_INF = -1e30

kernel_name = 'hymba_moba_ssd_convffn'


def rms_norm(x, g):
    xf = x.astype(jnp.float32)
    y = xf * lax.rsqrt(jnp.mean(xf * xf, axis=-1, keepdims=True) + NORM_EPS)
    return (y * g.astype(jnp.float32)).astype(x.dtype)


def gated_group_rms_norm(y, z, g):
    b, s, w = y.shape
    h = (y.astype(jnp.float32) * jax.nn.silu(z.astype(jnp.float32))).reshape(b, s, SSM_GROUPS, w // SSM_GROUPS)
    h = h * lax.rsqrt(jnp.mean(h * h, axis=-1, keepdims=True) + NORM_EPS)
    return (h.reshape(b, s, w) * g.astype(jnp.float32)).astype(y.dtype)


def causal_dwconv(x, w, b):
    width = w.shape[0]
    out = lax.conv_general_dilated(
        x, w[:, None, :].astype(x.dtype), window_strides=(1,), padding=[(width - 1, 0)],
        dimension_numbers=('NWC', 'WIO', 'NWC'), feature_group_count=x.shape[-1])
    return out + b.astype(x.dtype)


def rope(x, positions):
    half = x.shape[-1] // 2
    inv_freq = jnp.power(ROPE_THETA, -jnp.arange(half, dtype=jnp.float32) / half)
    ang = positions.astype(jnp.float32)[:, None] * inv_freq[None, :]
    cos, sin = jnp.cos(ang), jnp.sin(ang)
    xf = x.astype(jnp.float32)
    x1, x2 = xf[..., :half], xf[..., half:]
    return jnp.concatenate([x1 * cos - x2 * sin, x2 * cos + x1 * sin], axis=-1).astype(x.dtype)


def moba_attention(q, k, v):
    bsz, nh, s, dh = q.shape
    nb = -(-s // MOBA_BLOCK)
    pad = nb * MOBA_BLOCK - s
    kp = jnp.pad(k, ((0, 0), (0, 0), (0, pad), (0, 0)))
    vp = jnp.pad(v, ((0, 0), (0, 0), (0, pad), (0, 0)))
    k_blk = kp.reshape(bsz, nh, nb, MOBA_BLOCK, dh)
    v_blk = vp.reshape(bsz, nh, nb, MOBA_BLOCK, dh)
    k_mean = jnp.mean(k_blk.astype(jnp.float32), axis=3)
    n_sel = min(MOBA_TOPK, nb)
    scale = dh ** -0.5
    n_chunks = s // MOBA_Q_CHUNK
    q_chunks = q.reshape(bsz, nh, n_chunks, MOBA_Q_CHUNK, dh).transpose(2, 0, 1, 3, 4)
    b_idx = jnp.arange(bsz)[:, None, None, None]
    h_idx = jnp.arange(nh)[None, :, None, None]
    blk_ids = jnp.arange(nb)

    def one_chunk(args):
        c, qc = args
        own = (c * MOBA_Q_CHUNK) // MOBA_BLOCK
        q_pos = c * MOBA_Q_CHUNK + jnp.arange(MOBA_Q_CHUNK)
        gate = jnp.einsum('bhqd,bhnd->bhqn', qc.astype(jnp.float32), k_mean)
        gate = jnp.where(blk_ids < own, gate, NEG_INF)
        _, sel = lax.top_k(gate, n_sel)
        slot_ok = jnp.arange(n_sel) < own
        k_own = lax.dynamic_index_in_dim(k_blk, own, axis=2, keepdims=False)
        v_own = lax.dynamic_index_in_dim(v_blk, own, axis=2, keepdims=False)
        key_pos = own * MOBA_BLOCK + jnp.arange(MOBA_BLOCK)
        s_own = jnp.einsum('bhqd,bhkd->bhqk', qc, k_own, preferred_element_type=jnp.float32) * scale
        s_own = jnp.where(key_pos[None, :] <= q_pos[:, None], s_own, NEG_INF)
        k_sel = k_blk[b_idx, h_idx, sel]
        v_sel = v_blk[b_idx, h_idx, sel]
        s_sel = jnp.einsum('bhqd,bhqjkd->bhqjk', qc, k_sel, preferred_element_type=jnp.float32) * scale
        s_sel = jnp.where(slot_ok[:, None], s_sel, NEG_INF)
        scores = jnp.concatenate([s_own, s_sel.reshape(bsz, nh, MOBA_Q_CHUNK, n_sel * MOBA_BLOCK)], axis=-1)
        p = jax.nn.softmax(scores, axis=-1)
        p_own = p[..., :MOBA_BLOCK].astype(v.dtype)
        p_sel = p[..., MOBA_BLOCK:].reshape(bsz, nh, MOBA_Q_CHUNK, n_sel, MOBA_BLOCK).astype(v.dtype)
        out = jnp.einsum('bhqk,bhkd->bhqd', p_own, v_own) + jnp.einsum('bhqjk,bhqjkd->bhqd', p_sel, v_sel)
        return out.astype(q.dtype)

    outs = lax.map(one_chunk, (jnp.arange(n_chunks, dtype=jnp.int32), q_chunks))
    return outs.transpose(1, 2, 0, 3, 4).reshape(bsz, nh, s, dh)


def ssd_chunked(x, dt, a_head, b_in, c_in):
    bsz, s, nh, p = x.shape
    g, n = b_in.shape[2], b_in.shape[3]
    r = nh // g
    L = SSD_CHUNK
    nc = -(-s // L)
    pad = nc * L - s

    def padseq(t):
        return jnp.pad(t, [(0, 0), (0, pad)] + [(0, 0)] * (t.ndim - 2))

    xdt = padseq(x.astype(jnp.float32) * dt[..., None]).reshape(bsz, nc, L, g, r, p)
    a = padseq(dt * a_head).reshape(bsz, nc, L, g, r).transpose(0, 3, 4, 1, 2)
    bb = padseq(b_in.astype(jnp.float32)).reshape(bsz, nc, L, g, n)
    cc = padseq(c_in.astype(jnp.float32)).reshape(bsz, nc, L, g, n)
    a_cs = jnp.cumsum(a, axis=-1)
    seg = a_cs[..., :, None] - a_cs[..., None, :]
    causal = jnp.tril(jnp.ones((L, L), dtype=bool))
    decay_in = jnp.where(causal, jnp.exp(jnp.where(causal, seg, 0.0)), 0.0)
    cb = jnp.einsum('bclgn,bcsgn->bgcls', cc, bb)
    y_diag = jnp.einsum('bgrcls,bcsgrp->bclgrp', cb[:, :, None] * decay_in, xdt)
    decay_to_end = jnp.exp(a_cs[..., -1:] - a_cs).transpose(0, 3, 4, 1, 2)
    states = jnp.einsum('bclgn,bclgrp->bcgrpn', bb, xdt * decay_to_end[..., None])
    chunk_decay = jnp.exp(a_cs[..., -1]).transpose(3, 0, 1, 2)

    def step(h, inp):
        st, dec = inp
        return dec[..., None, None] * h + st, h

    h0 = jnp.zeros((bsz, g, r, p, n), jnp.float32)
    _, prev = lax.scan(step, h0, (states.transpose(1, 0, 2, 3, 4, 5), chunk_decay))
    decay_from_start = jnp.exp(a_cs).transpose(0, 3, 4, 1, 2)
    y_off = jnp.einsum('bclgn,cbgrpn->bclgrp', cc, prev) * decay_from_start[..., None]
    return (y_diag + y_off).reshape(bsz, nc * L, nh, p)[:, :s]


def moba_branch(q, k, v, gain):
    bsz, s, _ = q.shape

    def to_heads(t):
        return t.reshape(bsz, s, N_ATTN_HEADS, ATTN_HEAD_DIM).transpose(0, 2, 1, 3)

    pos = jnp.arange(s)
    o = moba_attention(rope(to_heads(q), pos), rope(to_heads(k), pos), to_heads(v))
    o = o.transpose(0, 2, 1, 3).reshape(bsz, s, ATTN_WIDTH)
    return rms_norm(o, gain)


def ssd_branch(z, xbc_raw, dt_raw, conv_w, conv_b, dt_bias, a_log, d_skip, norm_w):
    bsz, s, _ = z.shape
    xbc = jax.nn.silu(causal_dwconv(xbc_raw, conv_w, conv_b))
    gn = SSM_GROUPS * SSM_STATE
    x_s, b_s, c_s = jnp.split(xbc, [SSM_WIDTH, SSM_WIDTH + gn], axis=-1)
    dt = jax.nn.softplus(dt_raw.astype(jnp.float32) + dt_bias.astype(jnp.float32))
    a_head = -jnp.exp(a_log.astype(jnp.float32))
    xh = x_s.reshape(bsz, s, N_SSM_HEADS, SSM_HEAD_DIM)
    y = ssd_chunked(xh, dt, a_head,
                    b_s.reshape(bsz, s, SSM_GROUPS, SSM_STATE),
                    c_s.reshape(bsz, s, SSM_GROUPS, SSM_STATE))
    y = y + d_skip.astype(jnp.float32)[:, None] * xh.astype(jnp.float32)
    y = y.reshape(bsz, s, SSM_WIDTH).astype(z.dtype)
    return gated_group_rms_norm(y, z, norm_w)


def conv_gated_mlp(h, w_up, conv_w, conv_b, w_down):
    up = causal_dwconv(h @ w_up, conv_w, conv_b)
    gate, val = jnp.split(up, 2, axis=-1)
    return (jax.nn.silu(gate) * val) @ w_down


def setup_inputs(seed: int = 0) -> dict:
    key = jax.random.key(seed)
    ks = jax.random.split(key, 17)
    f32 = jnp.float32

    def nrm(k, shape, scale):
        return jax.random.normal(k, shape, f32) * scale

    def gain(k, shape):
        return 1.0 + 0.02 * jax.random.normal(k, shape, f32)

    dt0 = jnp.exp(jax.random.uniform(ks[6], (DEPTH, N_SSM_HEADS), f32, math.log(DT_MIN), math.log(DT_MAX)))
    return {
        'x': nrm(ks[0], (BATCH, SEQ, D_MODEL), 1.0),
        'ln1': gain(ks[1], (DEPTH, D_MODEL)),
        'w_in': nrm(ks[2], (DEPTH, D_MODEL, IN_WIDTH), D_MODEL ** -0.5),
        'attn_norm': gain(ks[3], (DEPTH, ATTN_WIDTH)),
        'ssm_conv_w': nrm(ks[4], (DEPTH, SSM_CONV, XBC_WIDTH), SSM_CONV ** -0.5),
        'ssm_conv_b': nrm(ks[5], (DEPTH, XBC_WIDTH), 0.02),
        'dt_bias': dt0 + jnp.log(-jnp.expm1(-dt0)),
        'a_log': jnp.log(jax.random.uniform(ks[7], (DEPTH, N_SSM_HEADS), f32, 1.0, 16.0)),
        'd_skip': gain(ks[8], (DEPTH, N_SSM_HEADS)),
        'ssm_norm': gain(ks[9], (DEPTH, SSM_WIDTH)),
        'w_out': nrm(ks[10], (DEPTH, MIX_WIDTH, D_MODEL), MIX_WIDTH ** -0.5),
        'ln2': gain(ks[11], (DEPTH, D_MODEL)),
        'w_up': nrm(ks[12], (DEPTH, D_MODEL, 2 * D_FF), D_MODEL ** -0.5),
        'ffn_conv_w': nrm(ks[13], (DEPTH, FFN_CONV, 2 * D_FF), FFN_CONV ** -0.5),
        'ffn_conv_b': nrm(ks[14], (DEPTH, 2 * D_FF), 0.02),
        'w_down': nrm(ks[15], (DEPTH, D_FF, D_MODEL), D_FF ** -0.5),
        'final_norm': gain(ks[16], (D_MODEL,)),
    }


def reference(x, ln1, w_in, attn_norm, ssm_conv_w, ssm_conv_b, dt_bias, a_log, d_skip,
              ssm_norm, w_out, ln2, w_up, ffn_conv_w, ffn_conv_b, w_down, final_norm):
    for i in range(DEPTH):
        h = rms_norm(x, ln1[i])
        proj = h @ w_in[i]
        q, k, v, z, xbc, dt_raw = jnp.split(proj, IN_OFFSETS, axis=-1)
        attn_out = moba_branch(q, k, v, attn_norm[i])
        ssm_out = ssd_branch(z, xbc, dt_raw, ssm_conv_w[i], ssm_conv_b[i], dt_bias[i],
                             a_log[i], d_skip[i], ssm_norm[i])
        x = x + jnp.concatenate([attn_out, ssm_out], axis=-1) @ w_out[i]
        h2 = rms_norm(x, ln2[i])
        x = x + conv_gated_mlp(h2, w_up[i], ffn_conv_w[i], ffn_conv_b[i], w_down[i])
    return rms_norm(x, final_norm)
```

```python
import functools

import numpy as np
import jax
import jax.numpy as jnp
from jax import lax
from jax.experimental import pallas as pl
from jax.experimental.pallas import tpu as pltpu

F32 = jnp.float32
BF16 = jnp.bfloat16

NORM_EPS = 1e-6
NEG_INF = -1e30
ROPE_THETA = 10000.0

ATTN_HEAD_DIM = 128
MOBA_BLOCK = 256
MOBA_TOPK = 3
MOBA_MAX_BLOCKS = 8

SSM_HEAD_DIM = 64
SSM_GROUPS = 8
SSM_STATE = 128
SSM_CONV = 4
SSD_CHUNK = 256
FFN_CONV = 3

LANES = 128
BF16_ROWS = 16
VMEM_LIMIT_BYTES = 56 * 1024 * 1024


def _cparams(*sem):
    return pltpu.CompilerParams(dimension_semantics=sem, vmem_limit_bytes=VMEM_LIMIT_BYTES)


def _row_tile(seq, pref):
    t = min(seq, pref)
    assert seq % t == 0
    return t


def _rms(xf, g):
    ms = jnp.mean(xf * xf, axis=-1, keepdims=True)
    return xf * lax.rsqrt(ms + NORM_EPS) * g


def _silu(x):
    return x * (1.0 / (1.0 + jnp.exp(-x)))


def _nt_dot(a, b):
    return lax.dot_general(a, b, (((1,), (1,)), ((), ())), preferred_element_type=F32)


def _proj_in_kernel(x_ref, ln_ref, w_ref, wdt_ref, cos_ref, sin_ref,
                    q_ref, k_ref, v_ref, z_ref, xbc_ref, dt_ref, h_ref, *, nq, nz, hpt):
    j = pl.program_id(1)

    @pl.when(j == 0)
    def _():
        h = _rms(x_ref[...], ln_ref[...]).astype(BF16)
        h_ref[...] = h
        dt_ref[...] = jnp.dot(h, wdt_ref[...], preferred_element_type=F32)

    acc = jnp.dot(h_ref[...], w_ref[...], preferred_element_type=F32)

    def rope_store(o_ref):
        cos = cos_ref[...]
        sin = sin_ref[...]
        for hh in range(hpt):
            a = acc[:, hh * LANES:(hh + 1) * LANES]
            o_ref[0, hh] = (a * cos + pltpu.roll(a, ATTN_HEAD_DIM // 2, axis=1) * sin).astype(BF16)

    @pl.when(j < nq)
    def _():
        rope_store(q_ref)

    @pl.when((j >= nq) & (j < 2 * nq))
    def _():
        rope_store(k_ref)

    @pl.when((j >= 2 * nq) & (j < 3 * nq))
    def _():
        for hh in range(hpt):
            v_ref[0, hh] = acc[:, hh * LANES:(hh + 1) * LANES].astype(BF16)

    @pl.when((j >= 3 * nq) & (j < 3 * nq + nz))
    def _():
        z_ref[...] = acc.astype(BF16)

    @pl.when(j >= 3 * nq + nz)
    def _():
        xbc_ref[...] = acc.astype(BF16)


def _proj_in(x2, ln, w_main, w_dt, cos_t, sin_t, *, batch, seq, attn_w, ssm_w, xbc_w):
    T, D = x2.shape
    tm = _row_tile(seq, 512)
    tn = 512
    nsb = seq // tm
    n_heads = attn_w // ATTN_HEAD_DIM
    hpt = tn // ATTN_HEAD_DIM
    nq = attn_w // tn
    nz = ssm_w // tn
    nx = xbc_w // tn
    nj = 3 * nq + nz + nx
    assert w_main.shape == (D, nj * tn)

    def clampj(lo, n):
        return lambda j: jnp.clip(j - lo, 0, n - 1)

    qj, kj, vj = clampj(0, nq), clampj(nq, nq), clampj(2 * nq, nq)
    zj, xj = clampj(3 * nq, nz), clampj(3 * nq + nz, nx)

    head_shape = jax.ShapeDtypeStruct((batch, n_heads, seq, ATTN_HEAD_DIM), BF16)

    def head_spec(fj):
        return pl.BlockSpec((1, hpt, tm, ATTN_HEAD_DIM), lambda i, j: (i // nsb, fj(j), i % nsb, 0))

    return pl.pallas_call(
        functools.partial(_proj_in_kernel, nq=nq, nz=nz, hpt=hpt),
        grid=(T // tm, nj),
        in_specs=[
            pl.BlockSpec((tm, D), lambda i, j: (i, 0)),
            pl.BlockSpec((1, D), lambda i, j: (0, 0)),
            pl.BlockSpec((D, tn), lambda i, j: (0, j)),
            pl.BlockSpec((D, LANES), lambda i, j: (0, 0)),
            pl.BlockSpec((tm, ATTN_HEAD_DIM), lambda i, j: (i % nsb, 0)),
            pl.BlockSpec((tm, ATTN_HEAD_DIM), lambda i, j: (i % nsb, 0)),
        ],
        out_specs=[
            head_spec(qj), head_spec(kj), head_spec(vj),
            pl.BlockSpec((tm, tn), lambda i, j: (i, zj(j))),
            pl.BlockSpec((tm, tn), lambda i, j: (i, xj(j))),
            pl.BlockSpec((tm, LANES), lambda i, j: (i, 0)),
        ],
        out_shape=[
            head_shape, head_shape, head_shape,
            jax.ShapeDtypeStruct((T, ssm_w), BF16),
            jax.ShapeDtypeStruct((T, xbc_w), BF16),
            jax.ShapeDtypeStruct((T, LANES), F32),
        ],
        scratch_shapes=[pltpu.VMEM((tm, D), BF16)],
        compiler_params=_cparams("parallel", "arbitrary"),
        name="proj_in",
    )(x2, ln, w_main, w_dt, cos_t, sin_t)


def _moba_kernel(q_ref, k_ref, v_ref, oh_ref, o_ref, qa_ref, *, nb, scale):
    i = pl.program_id(2)
    seq = nb * MOBA_BLOCK
    nbp = MOBA_MAX_BLOCKS

    @pl.when(i == 0)
    def _():
        kf = k_ref[0, 0].astype(F32)
        rows = [jnp.sum(kf[b * MOBA_BLOCK:(b + 1) * MOBA_BLOCK], axis=0, keepdims=True) for b in range(nb)]
        if nb < nbp:
            rows.append(jnp.zeros((nbp - nb, ATTN_HEAD_DIM), F32))
        kmean = jnp.concatenate(rows, axis=0) * (1.0 / MOBA_BLOCK)
        k_hi = kmean.astype(BF16)
        k_lo = (kmean - k_hi.astype(F32)).astype(BF16)
        q = q_ref[0, 0]
        g2 = _nt_dot(jnp.concatenate([k_hi, k_lo], axis=0), q)
        gate = g2[0:nbp] + g2[nbp:2 * nbp]
        own = jnp.right_shift(lax.broadcasted_iota(jnp.int32, (nbp, seq), 1), MOBA_BLOCK.bit_length() - 1)
        blk = lax.broadcasted_iota(jnp.int32, (nbp, seq), 0)
        rank = jnp.zeros((nbp, seq), F32)
        for b in range(nb):
            gb = gate[b:b + 1, :]
            beats = (b < own) & ((gb > gate) | ((gb == gate) & (b < blk)))
            rank = rank + jnp.where(beats, 1.0, 0.0)
        allowed = ((blk < own) & (rank < MOBA_TOPK)) | (blk == own)
        bias_t = jnp.where(allowed, 0.0, NEG_INF)
        bias_t = jnp.concatenate([bias_t, jnp.zeros((LANES - nbp, seq), F32)], axis=0)
        qa_ref[:, 0:ATTN_HEAD_DIM] = q
        qa_ref[:, ATTN_HEAD_DIM:] = bias_t.T.astype(BF16)

    row0 = pl.multiple_of(i * MOBA_BLOCK, MOBA_BLOCK)
    qa = qa_ref[pl.ds(row0, MOBA_BLOCK), :]

    def scores(b):
        r0 = pl.multiple_of(b * MOBA_BLOCK, MOBA_BLOCK)
        ka = jnp.concatenate([k_ref[0, 0, pl.ds(r0, MOBA_BLOCK), :], oh_ref[b]], axis=1)
        return _nt_dot(qa, ka) * scale, v_ref[0, 0, pl.ds(r0, MOBA_BLOCK), :]

    s, v_own = scores(i)
    qi = lax.broadcasted_iota(jnp.int32, (MOBA_BLOCK, MOBA_BLOCK), 0)
    ki = lax.broadcasted_iota(jnp.int32, (MOBA_BLOCK, MOBA_BLOCK), 1)
    s = jnp.where(ki <= qi, s, NEG_INF)
    m = jnp.max(s, axis=1, keepdims=True)
    p = jnp.exp(s - m)
    l = jnp.sum(p, axis=1, keepdims=True)
    acc = jnp.dot(p.astype(BF16), v_own, preferred_element_type=F32)

    def body(b, carry):
        m, l, acc = carry
        s, vb = scores(b)
        m_new = jnp.maximum(m, jnp.max(s, axis=1, keepdims=True))
        alpha = jnp.exp(m - m_new)
        p = jnp.exp(s - m_new)
        l = alpha * l + jnp.sum(p, axis=1, keepdims=True)
        acc = alpha * acc + jnp.dot(p.astype(BF16), vb, preferred_element_type=F32)
        return m_new, l, acc

    m, l, acc = lax.fori_loop(0, i, body, (m, l, acc))
    o_ref[0, 0] = (acc / l).astype(BF16)


def _moba(q, k, v, onehot):
    batch, n_heads, seq, dh = q.shape
    nb = seq // MOBA_BLOCK
    assert seq % MOBA_BLOCK == 0 and nb <= MOBA_MAX_BLOCKS and dh == ATTN_HEAD_DIM
    full = pl.BlockSpec((1, 1, seq, dh), lambda b, h, i: (b, h, 0, 0))
    return pl.pallas_call(
        functools.partial(_moba_kernel, nb=nb, scale=dh ** -0.5),
        grid=(batch, n_heads, nb),
        in_specs=[full, full, full,
                  pl.BlockSpec((MOBA_MAX_BLOCKS, MOBA_BLOCK, LANES), lambda b, h, i: (0, 0, 0))],
        out_specs=pl.BlockSpec((1, 1, MOBA_BLOCK, dh), lambda b, h, i: (b, h, i, 0)),
        out_shape=jax.ShapeDtypeStruct((batch, n_heads, seq, dh), BF16),
        scratch_shapes=[pltpu.VMEM((seq, 2 * dh), BF16)],
        compiler_params=_cparams("parallel", "parallel", "arbitrary"),
        name="moba",
    )(q, k, v, onehot)


def _ssd_kernel(xx_ref, xb_ref, xc_ref, hx_ref, hb_ref, hc_ref, cw_ref, cb_ref,
                dt_ref, dtb_ref, alog_ref, dskip_ref, z_ref, nw_ref, tri_ref, ltri_ref,
                y_ref,
                st_ref, acsc_ref, dtc_ref, rows_ref, acst_ref, dtt_ref, ext_ref, *, hpg):
    c = pl.program_id(1)
    g = pl.program_id(2)
    L = SSD_CHUNK
    P = SSM_HEAD_DIM
    N = SSM_STATE
    gw = hpg * P
    n_groups = st_ref.shape[0]

    @pl.when(g == 0)
    def _():
        dtv = dt_ref[0] + dtb_ref[...]
        dtv = jnp.maximum(dtv, 0.0) + jnp.log1p(jnp.exp(-jnp.abs(dtv)))
        a = dtv * (-jnp.exp(alog_ref[...]))
        a_hi = a.astype(BF16)
        r1 = a - a_hi.astype(F32)
        a_mid = r1.astype(BF16)
        a_lo = (r1 - a_mid.astype(F32)).astype(BF16)
        cs3 = jnp.dot(ltri_ref[...], jnp.concatenate([a_hi, a_mid, a_lo], axis=1),
                      preferred_element_type=F32)
        acs = cs3[:, 0:LANES] + cs3[:, LANES:2 * LANES] + cs3[:, 2 * LANES:]
        acst_ref[...] = acs.T
        dtt_ref[...] = dtv.T
        for gg in range(n_groups):
            sh = (LANES - hpg * gg) % LANES
            acsc_ref[gg] = pltpu.roll(acs, sh, axis=1) if sh else acs
            dtc_ref[gg] = pltpu.roll(dtv, sh, axis=1) if sh else dtv
            rows_ref[gg, 0:hpg, :] = acst_ref[hpg * gg:hpg * (gg + 1), :]
            rows_ref[gg, hpg:2 * hpg, :] = dtt_ref[hpg * gg:hpg * (gg + 1), :]

    @pl.when(c == 0)
    def _():
        st_ref[g] = jnp.zeros((N, gw), F32)

    live = jnp.where(c > 0, 1.0, 0.0)
    hr = BF16_ROWS
    ext_ref[0:hr, 0:gw] = hx_ref[0].astype(F32) * live
    ext_ref[0:hr, gw:gw + N] = hb_ref[0].astype(F32) * live
    ext_ref[0:hr, gw + N:] = hc_ref[0].astype(F32) * live
    ext_ref[hr:, 0:gw] = xx_ref[0].astype(F32)
    ext_ref[hr:, gw:gw + N] = xb_ref[0].astype(F32)
    ext_ref[hr:, gw + N:] = xc_ref[0].astype(F32)
    cw = cw_ref[0]
    conv = cb_ref[0]
    for t in range(SSM_CONV):
        o = hr - (SSM_CONV - 1) + t
        conv = conv + cw[t:t + 1, :] * ext_ref[o:o + L, :]
    act = _silu(conv)
    xs = act[:, 0:gw]
    bm = act[:, gw:gw + N]
    cm_b = act[:, gw + N:].astype(BF16)

    cb_mat = _nt_dot(cm_b, bm.astype(BF16))
    bm_t = bm.T
    acsc = acsc_ref[g]
    tri = tri_ref[...]
    lane_head = jnp.right_shift(lax.broadcasted_iota(jnp.int32, (1, gw), 1), P.bit_length() - 1)
    e_col = jnp.exp(acsc)

    mp, bw, xm = [], [], []
    dfs = jnp.zeros((L, gw), F32)
    dch = jnp.zeros((1, gw), F32)
    for r in range(hpg):
        a_col = acsc[:, r:r + 1]
        a_row = rows_ref[g, r:r + 1, :]
        d_row = rows_ref[g, hpg + r:hpg + r + 1, :]
        dec = jnp.exp(a_col - a_row + tri)
        mp.append((cb_mat * dec * d_row).astype(BF16))
        a_last = a_row[:, L - 1:L]
        bw.append((bm_t * (d_row * jnp.exp(a_last - a_row))).astype(BF16))
        hmask = jnp.where(lane_head == r, 1.0, 0.0)
        xm.append((xs * hmask).astype(BF16))
        dfs = dfs + e_col[:, r:r + 1] * hmask
        dch = dch + jnp.exp(a_last) * hmask
    mp = jnp.concatenate(mp, axis=1)
    bw = jnp.concatenate(bw, axis=1)
    xm = jnp.concatenate(xm, axis=0)

    st_old = st_ref[g]
    y = jnp.dot(mp, xm, preferred_element_type=F32)
    y = y + jnp.dot(cm_b, st_old.astype(BF16), preferred_element_type=F32) * dfs
    y = y + dskip_ref[...] * xs
    st_ref[g] = st_old * dch + jnp.dot(bw, xm, preferred_element_type=F32)

    hg = y * _silu(z_ref[0].astype(F32))
    y_ref[0] = (_rms(hg, nw_ref[...])).astype(BF16)


def _ssd(xbc3, dt3, z3, cw_g, cb_g, dtb, alog, dskip, nw, tri, ltri, *, ssm_w):
    batch, seq, xbc_w = xbc3.shape
    L = SSD_CHUNK
    N = SSM_STATE
    G = SSM_GROUPS
    assert seq % L == 0
    nc = seq // L
    gw = ssm_w // G
    hpg = gw // SSM_HEAD_DIM
    assert 2 * hpg <= 8 and xbc_w == ssm_w + 2 * G * N and gw % LANES == 0
    b_off = ssm_w // N
    c_off = b_off + G
    hb = L // BF16_ROWS

    def halo(c):
        return jnp.maximum(c * hb - 1, 0)

    return pl.pallas_call(
        functools.partial(_ssd_kernel, hpg=hpg),
        grid=(batch, nc, G),
        in_specs=[
            pl.BlockSpec((1, L, gw), lambda b, c, g: (b, c, g)),
            pl.BlockSpec((1, L, N), lambda b, c, g: (b, c, b_off + g)),
            pl.BlockSpec((1, L, N), lambda b, c, g: (b, c, c_off + g)),
            pl.BlockSpec((1, BF16_ROWS, gw), lambda b, c, g: (b, halo(c), g)),
            pl.BlockSpec((1, BF16_ROWS, N), lambda b, c, g: (b, halo(c), b_off + g)),
            pl.BlockSpec((1, BF16_ROWS, N), lambda b, c, g: (b, halo(c), c_off + g)),
            pl.BlockSpec((1, SSM_CONV, gw + 2 * N), lambda b, c, g: (g, 0, 0)),
            pl.BlockSpec((1, 1, gw + 2 * N), lambda b, c, g: (g, 0, 0)),
            pl.BlockSpec((1, L, LANES), lambda b, c, g: (b, c, 0)),
            pl.BlockSpec((1, LANES), lambda b, c, g: (0, 0)),
            pl.BlockSpec((1, LANES), lambda b, c, g: (0, 0)),
            pl.BlockSpec((1, gw), lambda b, c, g: (0, g)),
            pl.BlockSpec((1, L, gw), lambda b, c, g: (b, c, g)),
            pl.BlockSpec((1, gw), lambda b, c, g: (0, g)),
            pl.BlockSpec((L, L), lambda b, c, g: (0, 0)),
            pl.BlockSpec((L, L), lambda b, c, g: (0, 0)),
        ],
        out_specs=pl.BlockSpec((1, L, gw), lambda b, c, g: (b, c, g)),
        out_shape=jax.ShapeDtypeStruct((batch, seq, ssm_w), BF16),
        scratch_shapes=[
            pltpu.VMEM((G, N, gw), F32),
            pltpu.VMEM((G, L, LANES), F32),
            pltpu.VMEM((G, L, LANES), F32),
            pltpu.VMEM((G, 8, L), F32),
            pltpu.VMEM((LANES, L), F32),
            pltpu.VMEM((LANES, L), F32),
            pltpu.VMEM((L + BF16_ROWS, gw + 2 * N), F32),
        ],
        compiler_params=_cparams("parallel", "arbitrary", "arbitrary"),
        name="ssd",
    )(xbc3, xbc3, xbc3, xbc3, xbc3, xbc3, cw_g, cb_g, dt3, dtb, alog, dskip, z3, nw, tri, ltri)


def _proj_out_kernel(attn_ref, y_ref, an_ref, w_ref, x_ref, o_ref, cat_ref, *, n_heads):
    j = pl.program_id(1)
    aw = n_heads * ATTN_HEAD_DIM

    @pl.when(j == 0)
    def _():
        a = jnp.concatenate([attn_ref[0, hh].astype(F32) for hh in range(n_heads)], axis=1)
        cat_ref[:, 0:aw] = _rms(a, an_ref[...]).astype(BF16)
        cat_ref[:, aw:] = y_ref[...]

    o_ref[...] = x_ref[...] + jnp.dot(cat_ref[...], w_ref[...], preferred_element_type=F32)


def _proj_out(attn, y2, an, w_out, x2, *, seq):
    batch, n_heads, _, dh = attn.shape
    T, D = x2.shape
    aw = n_heads * dh
    sw = y2.shape[1]
    tm = _row_tile(seq, 512)
    tn = 512
    nsb = seq // tm
    return pl.pallas_call(
        functools.partial(_proj_out_kernel, n_heads=n_heads),
        grid=(T // tm, D // tn),
        in_specs=[
            pl.BlockSpec((1, n_heads, tm, dh), lambda i, j: (i // nsb, 0, i % nsb, 0)),
            pl.BlockSpec((tm, sw), lambda i, j: (i, 0)),
            pl.BlockSpec((1, aw), lambda i, j: (0, 0)),
            pl.BlockSpec((aw + sw, tn), lambda i, j: (0, j)),
            pl.BlockSpec((tm, tn), lambda i, j: (i, j)),
        ],
        out_specs=pl.BlockSpec((tm, tn), lambda i, j: (i, j)),
        out_shape=jax.ShapeDtypeStruct((T, D), F32),
        scratch_shapes=[pltpu.VMEM((tm, aw + sw), BF16)],
        compiler_params=_cparams("parallel", "arbitrary"),
        name="proj_out",
    )(attn, y2, an, w_out, x2)


def _ffn_kernel(x_ref, xh_ref, ln_ref, wg_ref, wv_ref, cwg_ref, cwv_ref, cbg_ref, cbv_ref,
                wd_ref, fn_ref, o_ref, h_ref, ug_ref, uv_ref, acc_ref, *, nsb, final):
    i = pl.program_id(0)
    j = pl.program_id(1)
    tm = x_ref.shape[0]
    hr = BF16_ROWS

    @pl.when(j == 0)
    def _():
        g = ln_ref[...]
        h_ref[hr:, :] = _rms(x_ref[...], g).astype(BF16)
        live = jnp.where(i % nsb != 0, 1.0, 0.0)
        h_ref[0:hr, :] = (_rms(xh_ref[...], g) * live).astype(BF16)
        acc_ref[...] = jnp.zeros_like(acc_ref)

    h = h_ref[...]
    ug_ref[...] = jnp.dot(h, wg_ref[...], preferred_element_type=F32)
    uv_ref[...] = jnp.dot(h, wv_ref[...], preferred_element_type=F32)

    def conv(u_ref, cw, cb):
        out = cb
        for t in range(FFN_CONV):
            o = hr - (FFN_CONV - 1) + t
            out = out + cw[t:t + 1, :] * u_ref[o:o + tm, :]
        return out

    cg = conv(ug_ref, cwg_ref[...], cbg_ref[...])
    cv = conv(uv_ref, cwv_ref[...], cbv_ref[...])
    act = (_silu(cg) * cv).astype(BF16)
    acc_ref[...] += jnp.dot(act, wd_ref[...], preferred_element_type=F32)

    @pl.when(j == pl.num_programs(1) - 1)
    def _():
        out = x_ref[...] + acc_ref[...]
        if final:
            out = _rms(out, fn_ref[...])
        o_ref[...] = out


def _ffn(x2, ln, w_up, conv_w, conv_b, w_down, fn, *, seq, final):
    T, D = x2.shape
    dff = w_down.shape[0]
    tm = _row_tile(seq, 512)
    tf = 512
    assert dff % tf == 0 and tm % BF16_ROWS == 0
    nsb = seq // tm
    nf = dff // tf
    hb = tm // BF16_ROWS
    return pl.pallas_call(
        functools.partial(_ffn_kernel, nsb=nsb, final=final),
        grid=(T // tm, nf),
        in_specs=[
            pl.BlockSpec((tm, D), lambda i, j: (i, 0)),
            pl.BlockSpec((BF16_ROWS, D), lambda i, j: (jnp.maximum(i * hb - 1, 0), 0)),
            pl.BlockSpec((1, D), lambda i, j: (0, 0)),
            pl.BlockSpec((D, tf), lambda i, j: (0, j)),
            pl.BlockSpec((D, tf), lambda i, j: (0, nf + j)),
            pl.BlockSpec((FFN_CONV, tf), lambda i, j: (0, j)),
            pl.BlockSpec((FFN_CONV, tf), lambda i, j: (0, nf + j)),
            pl.BlockSpec((1, tf), lambda i, j: (0, j)),
            pl.BlockSpec((1, tf), lambda i, j: (0, nf + j)),
            pl.BlockSpec((tf, D), lambda i, j: (j, 0)),
            pl.BlockSpec((1, D), lambda i, j: (0, 0)),
        ],
        out_specs=pl.BlockSpec((tm, D), lambda i, j: (i, 0)),
        out_shape=jax.ShapeDtypeStruct((T, D), F32),
        scratch_shapes=[
            pltpu.VMEM((tm + BF16_ROWS, D), BF16),
            pltpu.VMEM((tm + BF16_ROWS, tf), F32),
            pltpu.VMEM((tm + BF16_ROWS, tf), F32),
            pltpu.VMEM((tm, D), F32),
        ],
        compiler_params=_cparams("parallel", "arbitrary"),
        name="ffn_final" if final else "ffn",
    )(x2, x2, ln, w_up, w_up, conv_w, conv_w, conv_b, conv_b, w_down, fn)


def _rope_tables(seq):
    half = ATTN_HEAD_DIM // 2
    inv_freq = jnp.power(ROPE_THETA, -jnp.arange(half, dtype=F32) / half)
    ang = jnp.arange(seq, dtype=F32)[:, None] * inv_freq[None, :]
    cos, sin = jnp.cos(ang), jnp.sin(ang)
    return jnp.concatenate([cos, cos], axis=-1), jnp.concatenate([-sin, sin], axis=-1)


def _chunk_constants():
    L = SSD_CHUNK
    low = np.tril(np.ones((L, L), np.float32))
    tri = jnp.asarray(np.where(low > 0, 0.0, NEG_INF).astype(np.float32))
    ltri = jnp.asarray(low).astype(BF16)
    oh = np.zeros((MOBA_MAX_BLOCKS, MOBA_BLOCK, LANES), np.float32)
    for b in range(MOBA_MAX_BLOCKS):
        oh[b, :, b] = 1.0
    return tri, ltri, jnp.asarray(oh).astype(BF16)


def _pad_lanes(v):
    return jnp.pad(v, (0, LANES - v.shape[0]))[None, :]


def kernel(x, ln1, w_in, attn_norm, ssm_conv_w, ssm_conv_b, dt_bias, a_log, d_skip, ssm_norm, w_out, ln2, w_up, ffn_conv_w, ffn_conv_b, w_down, final_norm):
    batch, seq, d_model = x.shape
    depth = ln1.shape[0]
    attn_w = attn_norm.shape[1]
    ssm_w = ssm_norm.shape[1]
    xbc_w = ssm_conv_w.shape[2]
    n_ssm_heads = a_log.shape[1]
    assert ssm_w // n_ssm_heads == SSM_HEAD_DIM and n_ssm_heads <= LANES
    main_w = 3 * attn_w + ssm_w + xbc_w
    G, N = SSM_GROUPS, SSM_STATE
    gw = ssm_w // G

    cos_t, sin_t = _rope_tables(seq)
    tri, ltri, onehot = _chunk_constants()

    x2 = x.reshape(batch * seq, d_model)
    for i in range(depth):
        w_main = w_in[i][:, :main_w].astype(BF16)
        w_dt = jnp.pad(w_in[i][:, main_w:], ((0, 0), (0, LANES - n_ssm_heads))).astype(BF16)
        q, k, v, z2, xbc2, dt2 = _proj_in(
            x2, ln1[i][None, :], w_main, w_dt, cos_t, sin_t,
            batch=batch, seq=seq, attn_w=attn_w, ssm_w=ssm_w, xbc_w=xbc_w)

        attn = _moba(q, k, v, onehot)

        cw = ssm_conv_w[i]
        cb = ssm_conv_b[i]
        cw_g = jnp.concatenate([
            cw[:, :ssm_w].reshape(SSM_CONV, G, gw),
            cw[:, ssm_w:ssm_w + G * N].reshape(SSM_CONV, G, N),
            cw[:, ssm_w + G * N:].reshape(SSM_CONV, G, N)], axis=-1).transpose(1, 0, 2)
        cb_g = jnp.concatenate([
            cb[:ssm_w].reshape(G, 1, gw),
            cb[ssm_w:ssm_w + G * N].reshape(G, 1, N),
            cb[ssm_w + G * N:].reshape(G, 1, N)], axis=-1)
        y3 = _ssd(
            xbc2.reshape(batch, seq, xbc_w), dt2.reshape(batch, seq, LANES),
            z2.reshape(batch, seq, ssm_w), cw_g, cb_g,
            _pad_lanes(dt_bias[i]), _pad_lanes(a_log[i]),
            jnp.repeat(d_skip[i], SSM_HEAD_DIM)[None, :], ssm_norm[i][None, :], tri, ltri,
            ssm_w=ssm_w)

        x2 = _proj_out(attn, y3.reshape(batch * seq, ssm_w), attn_norm[i][None, :],
                       w_out[i].astype(BF16), x2, seq=seq)

        x2 = _ffn(x2, ln2[i][None, :], w_up[i].astype(BF16), ffn_conv_w[i], ffn_conv_b[i][None, :],
                  w_down[i].astype(BF16), final_norm[None, :], seq=seq, final=(i == depth - 1))
    return x2.reshape(batch, seq, d_model)
```

```python
import functools

import numpy as np
import jax
import jax.numpy as jnp
from jax import lax
from jax.experimental import pallas as pl
from jax.experimental.pallas import tpu as pltpu

F32 = jnp.float32
BF16 = jnp.bfloat16

NORM_EPS = 1e-6
NEG_INF = -1e30
LOG2E = 1.4426950408889634
ROPE_THETA = 10000.0

ATTN_HEAD_DIM = 128
MOBA_BLOCK = 256
MOBA_TOPK = 3
MOBA_MAX_BLOCKS = 8

SSM_HEAD_DIM = 64
SSM_GROUPS = 8
SSM_STATE = 128
SSM_CONV = 4
SSD_CHUNK = 256
FFN_CONV = 3

LANES = 128
BF16_ROWS = 16
VMEM_LIMIT_BYTES = 56 * 1024 * 1024


def _cparams(*sem):
    return pltpu.CompilerParams(dimension_semantics=sem, vmem_limit_bytes=VMEM_LIMIT_BYTES)


def _row_tile(seq, pref):
    t = min(seq, pref)
    assert seq % t == 0
    return t


def _rms(xf, g):
    ms = jnp.mean(xf * xf, axis=-1, keepdims=True)
    return xf * lax.rsqrt(ms + NORM_EPS) * g


def _silu(x):
    return x * (1.0 / (1.0 + jnp.exp(-x)))


def _nt_dot(a, b):
    return lax.dot_general(a, b, (((1,), (1,)), ((), ())), preferred_element_type=F32)


def _proj_in_kernel(x_ref, ln_ref, w_ref, wdt_ref, cos_ref, sin_ref, cw_ref, cb_ref,
                    q_ref, k_ref, v_ref, z_ref, xbc_ref, dt_ref,
                    h_ref, ext_ref, halo_ref, *, nq, nz, hpt, nsb):
    i = pl.program_id(0)
    j = pl.program_id(1)
    tm = x_ref.shape[0]

    @pl.when(j == 0)
    def _():
        h = _rms(x_ref[...], ln_ref[...]).astype(BF16)
        h_ref[...] = h
        dt_ref[...] = jnp.dot(h, wdt_ref[...], preferred_element_type=F32)

    def mm():
        return jnp.dot(h_ref[...], w_ref[...], preferred_element_type=F32)

    def rope_store(o_ref):
        acc = mm()
        cos = cos_ref[...]
        sin = sin_ref[...]
        for hh in range(hpt):
            a = acc[:, hh * LANES:(hh + 1) * LANES]
            o_ref[0, hh] = (a * cos + pltpu.roll(a, ATTN_HEAD_DIM // 2, axis=1) * sin).astype(BF16)

    @pl.when(j < nq)
    def _():
        rope_store(q_ref)

    @pl.when((j >= nq) & (j < 2 * nq))
    def _():
        rope_store(k_ref)

    @pl.when((j >= 2 * nq) & (j < 3 * nq))
    def _():
        acc = mm()
        for hh in range(hpt):
            v_ref[0, hh] = acc[:, hh * LANES:(hh + 1) * LANES].astype(BF16)

    @pl.when((j >= 3 * nq) & (j < 3 * nq + nz))
    def _():
        z_ref[...] = _silu(mm()).astype(BF16)

    @pl.when(j >= 3 * nq + nz)
    def _():
        jx = j - (3 * nq + nz)

        @pl.when(i % nsb == 0)
        def _():
            ext_ref[0:8, :] = jnp.zeros((8, ext_ref.shape[1]), F32)

        @pl.when(i % nsb != 0)
        def _():
            ext_ref[0:8, :] = halo_ref[jx]

        acc = mm()
        ext_ref[8:, :] = acc
        halo_ref[jx] = acc[tm - 8:, :]
        cw = cw_ref[...]
        conv = cb_ref[...]
        for t in range(SSM_CONV):
            o = 8 - (SSM_CONV - 1) + t
            conv = conv + cw[t:t + 1, :] * ext_ref[o:o + tm, :]
        xbc_ref[...] = _silu(conv).astype(BF16)


def _proj_in(x2, ln, w_main, w_dt, cos_t, sin_t, conv_w, conv_b, *, batch, seq, attn_w, ssm_w, xbc_w):
    T, D = x2.shape
    tm = _row_tile(seq, 512)
    tn = 512
    nsb = seq // tm
    n_heads = attn_w // ATTN_HEAD_DIM
    hpt = tn // ATTN_HEAD_DIM
    nq = attn_w // tn
    nz = ssm_w // tn
    nx = xbc_w // tn
    nj = 3 * nq + nz + nx
    assert w_main.shape == (D, nj * tn)

    def clampj(lo, n):
        return lambda j: jnp.clip(j - lo, 0, n - 1)

    qj, kj, vj = clampj(0, nq), clampj(nq, nq), clampj(2 * nq, nq)
    zj, xj = clampj(3 * nq, nz), clampj(3 * nq + nz, nx)

    head_shape = jax.ShapeDtypeStruct((batch, n_heads, seq, ATTN_HEAD_DIM), BF16)

    def head_spec(fj):
        return pl.BlockSpec((1, hpt, tm, ATTN_HEAD_DIM), lambda i, j: (i // nsb, fj(j), i % nsb, 0))

    return pl.pallas_call(
        functools.partial(_proj_in_kernel, nq=nq, nz=nz, hpt=hpt, nsb=nsb),
        grid=(T // tm, nj),
        in_specs=[
            pl.BlockSpec((tm, D), lambda i, j: (i, 0)),
            pl.BlockSpec((1, D), lambda i, j: (0, 0)),
            pl.BlockSpec((D, tn), lambda i, j: (0, j)),
            pl.BlockSpec((D, LANES), lambda i, j: (0, 0)),
            pl.BlockSpec((tm, ATTN_HEAD_DIM), lambda i, j: (i % nsb, 0)),
            pl.BlockSpec((tm, ATTN_HEAD_DIM), lambda i, j: (i % nsb, 0)),
            pl.BlockSpec((SSM_CONV, tn), lambda i, j: (0, xj(j))),
            pl.BlockSpec((1, tn), lambda i, j: (0, xj(j))),
        ],
        out_specs=[
            head_spec(qj), head_spec(kj), head_spec(vj),
            pl.BlockSpec((tm, tn), lambda i, j: (i, zj(j))),
            pl.BlockSpec((tm, tn), lambda i, j: (i, xj(j))),
            pl.BlockSpec((tm, LANES), lambda i, j: (i, 0)),
        ],
        out_shape=[
            head_shape, head_shape, head_shape,
            jax.ShapeDtypeStruct((T, ssm_w), BF16),
            jax.ShapeDtypeStruct((T, xbc_w), BF16),
            jax.ShapeDtypeStruct((T, LANES), F32),
        ],
        scratch_shapes=[
            pltpu.VMEM((tm, D), BF16),
            pltpu.VMEM((tm + 8, tn), F32),
            pltpu.VMEM((nx, 8, tn), F32),
        ],
        compiler_params=_cparams("arbitrary", "arbitrary"),
        name="proj_in",
    )(x2, ln, w_main, w_dt, cos_t, sin_t, conv_w, conv_b)


def _moba_kernel(q_ref, k_ref, v_ref, oh_ref, o_ref, qa_ref, *, nb, scale):
    seq = nb * MOBA_BLOCK
    nbp = MOBA_MAX_BLOCKS
    blk = MOBA_BLOCK

    kf = k_ref[0, 0].astype(F32)
    rows = [jnp.sum(kf[b * blk:(b + 1) * blk], axis=0, keepdims=True) for b in range(nb)]
    if nb < nbp:
        rows.append(jnp.zeros((nbp - nb, ATTN_HEAD_DIM), F32))
    kmean = jnp.concatenate(rows, axis=0) * (1.0 / blk)
    k_hi = kmean.astype(BF16)
    k_lo = (kmean - k_hi.astype(F32)).astype(BF16)
    q = q_ref[0, 0]
    g2 = _nt_dot(jnp.concatenate([k_hi, k_lo], axis=0), q)
    gate = g2[0:nbp] + g2[nbp:2 * nbp]
    own = jnp.right_shift(lax.broadcasted_iota(jnp.int32, (nbp, seq), 1), blk.bit_length() - 1)
    kb = lax.broadcasted_iota(jnp.int32, (nbp, seq), 0)
    rank = jnp.zeros((nbp, seq), F32)
    for b in range(nb):
        gb = gate[b:b + 1, :]
        beats = (b < own) & ((gb > gate) | ((gb == gate) & (b < kb)))
        rank = rank + jnp.where(beats, 1.0, 0.0)
    allowed = ((kb < own) & (rank < MOBA_TOPK)) | (kb == own)
    bias_t = jnp.where(allowed, 0.0, NEG_INF)
    bias_t = jnp.concatenate([bias_t, jnp.zeros((LANES - nbp, seq), F32)], axis=0)
    qa_ref[:, 0:ATTN_HEAD_DIM] = q
    qa_ref[:, ATTN_HEAD_DIM:] = bias_t.T.astype(BF16)

    qi = lax.broadcasted_iota(jnp.int32, (blk, blk), 0)
    ki = lax.broadcasted_iota(jnp.int32, (blk, blk), 1)
    causal = ki <= qi
    c2 = scale * LOG2E
    for i in range(nb):
        hi = (i + 1) * blk
        qa = qa_ref[i * blk:hi, :]
        ka = jnp.concatenate([k_ref[0, 0, 0:hi, :], oh_ref[0:hi, :]], axis=1)
        s = _nt_dot(qa, ka)
        s_own = jnp.where(causal, s[:, i * blk:], NEG_INF)
        s = jnp.concatenate([s[:, 0:i * blk], s_own], axis=1) if i else s_own
        m = jnp.max(s, axis=1, keepdims=True)
        p = jnp.exp2((s - m) * c2)
        l = jnp.sum(p, axis=1, keepdims=True)
        acc = jnp.dot(p.astype(BF16), v_ref[0, 0, 0:hi, :], preferred_element_type=F32)
        o_ref[0, 0, i * blk:hi, :] = (acc / l).astype(BF16)


def _moba(q, k, v, onehot):
    batch, n_heads, seq, dh = q.shape
    nb = seq // MOBA_BLOCK
    assert seq % MOBA_BLOCK == 0 and nb <= MOBA_MAX_BLOCKS and dh == ATTN_HEAD_DIM
    full = pl.BlockSpec((1, 1, seq, dh), lambda b, h: (b, h, 0, 0))
    return pl.pallas_call(
        functools.partial(_moba_kernel, nb=nb, scale=dh ** -0.5),
        grid=(batch, n_heads),
        in_specs=[full, full, full, pl.BlockSpec((seq, LANES), lambda b, h: (0, 0))],
        out_specs=full,
        out_shape=jax.ShapeDtypeStruct((batch, n_heads, seq, dh), BF16),
        scratch_shapes=[pltpu.VMEM((seq, 2 * dh), BF16)],
        compiler_params=_cparams("parallel", "parallel"),
        name="moba",
    )(q, k, v, onehot)


def _ssd_kernel(xx_ref, xb_ref, xc_ref, dt_ref, dtb_ref, alog_ref, dskip_ref, z_ref, nw_ref,
                tri_ref, ltri_ref, y_ref,
                st_ref, acsc_ref, rows_ref, rowt_ref, *, hpg):
    c = pl.program_id(1)
    g = pl.program_id(2)
    L = SSD_CHUNK
    P = SSM_HEAD_DIM
    N = SSM_STATE
    gw = hpg * P
    n_groups = st_ref.shape[0]

    @pl.when(g == 0)
    def _():
        dtv = dt_ref[0] + dtb_ref[...]
        dtv = jnp.maximum(dtv, 0.0) + jnp.log1p(jnp.exp(-jnp.abs(dtv)))
        a = dtv * (-jnp.exp(alog_ref[...]))
        a_hi = a.astype(BF16)
        r1 = a - a_hi.astype(F32)
        a_mid = r1.astype(BF16)
        a_lo = (r1 - a_mid.astype(F32)).astype(BF16)
        cs3 = jnp.dot(ltri_ref[...], jnp.concatenate([a_hi, a_mid, a_lo], axis=1),
                      preferred_element_type=F32)
        acs2 = (cs3[:, 0:LANES] + cs3[:, LANES:2 * LANES] + cs3[:, 2 * LANES:]) * LOG2E
        rowt_ref[...] = (acs2 - jnp.log2(dtv)).T
        for gg in range(n_groups):
            sh = (LANES - hpg * gg) % LANES
            acsc_ref[gg] = pltpu.roll(acs2, sh, axis=1) if sh else acs2
            rows_ref[gg, 0:hpg, :] = rowt_ref[hpg * gg:hpg * (gg + 1), :]

    @pl.when(c == 0)
    def _():
        st_ref[g] = jnp.zeros((N, gw), F32)

    xs_b = xx_ref[0]
    bm_b = xb_ref[0]
    cm_b = xc_ref[0]
    cb_mat = _nt_dot(cm_b, bm_b)
    bm_t = bm_b.astype(F32).T
    acsc = acsc_ref[g]
    tri = tri_ref[...]
    lane_head = jnp.right_shift(lax.broadcasted_iota(jnp.int32, (1, gw), 1), P.bit_length() - 1)
    e_col = jnp.exp2(acsc)
    a_end = acsc[L - 1:L, :]

    mp, bw, xm = [], [], []
    dfs = jnp.zeros((L, gw), F32)
    dch = jnp.zeros((1, gw), F32)
    for r in range(hpg):
        a_col = acsc[:, r:r + 1]
        a_row = rows_ref[g, r:r + 1, :]
        a_last = a_end[:, r:r + 1]
        mp.append((cb_mat * jnp.exp2(a_col - a_row + tri)).astype(BF16))
        bw.append((bm_t * jnp.exp2(a_last - a_row)).astype(BF16))
        hmask = lane_head == r
        xm.append(jnp.where(hmask, xs_b, jnp.zeros_like(xs_b)))
        hmf = jnp.where(hmask, 1.0, 0.0)
        dfs = dfs + e_col[:, r:r + 1] * hmf
        dch = dch + jnp.exp2(a_last) * hmf
    mp = jnp.concatenate(mp, axis=1)
    bw = jnp.concatenate(bw, axis=1)
    xm = jnp.concatenate(xm, axis=0)

    st_old = st_ref[g]
    y = jnp.dot(mp, xm, preferred_element_type=F32)
    y = y + jnp.dot(cm_b, st_old.astype(BF16), preferred_element_type=F32) * dfs
    y = y + dskip_ref[...] * xs_b.astype(F32)
    st_ref[g] = st_old * dch + jnp.dot(bw, xm, preferred_element_type=F32)

    hg = y * z_ref[0].astype(F32)
    y_ref[0] = (_rms(hg, nw_ref[...])).astype(BF16)


def _ssd(xbc3, dt3, z3, dtb, alog, dskip, nw, tri, ltri, *, ssm_w):
    batch, seq, xbc_w = xbc3.shape
    L = SSD_CHUNK
    N = SSM_STATE
    G = SSM_GROUPS
    assert seq % L == 0
    nc = seq // L
    gw = ssm_w // G
    hpg = gw // SSM_HEAD_DIM
    assert hpg <= 8 and xbc_w == ssm_w + 2 * G * N and gw % LANES == 0
    b_off = ssm_w // N
    c_off = b_off + G

    return pl.pallas_call(
        functools.partial(_ssd_kernel, hpg=hpg),
        grid=(batch, nc, G),
        in_specs=[
            pl.BlockSpec((1, L, gw), lambda b, c, g: (b, c, g)),
            pl.BlockSpec((1, L, N), lambda b, c, g: (b, c, b_off + g)),
            pl.BlockSpec((1, L, N), lambda b, c, g: (b, c, c_off + g)),
            pl.BlockSpec((1, L, LANES), lambda b, c, g: (b, c, 0)),
            pl.BlockSpec((1, LANES), lambda b, c, g: (0, 0)),
            pl.BlockSpec((1, LANES), lambda b, c, g: (0, 0)),
            pl.BlockSpec((1, gw), lambda b, c, g: (0, g)),
            pl.BlockSpec((1, L, gw), lambda b, c, g: (b, c, g)),
            pl.BlockSpec((1, gw), lambda b, c, g: (0, g)),
            pl.BlockSpec((L, L), lambda b, c, g: (0, 0)),
            pl.BlockSpec((L, L), lambda b, c, g: (0, 0)),
        ],
        out_specs=pl.BlockSpec((1, L, gw), lambda b, c, g: (b, c, g)),
        out_shape=jax.ShapeDtypeStruct((batch, seq, ssm_w), BF16),
        scratch_shapes=[
            pltpu.VMEM((G, N, gw), F32),
            pltpu.VMEM((G, L, LANES), F32),
            pltpu.VMEM((G, 8, L), F32),
            pltpu.VMEM((LANES, L), F32),
        ],
        compiler_params=_cparams("parallel", "arbitrary", "arbitrary"),
        name="ssd",
    )(xbc3, xbc3, xbc3, dt3, dtb, alog, dskip, z3, nw, tri, ltri)


def _proj_out_kernel(attn_ref, y_ref, an_ref, w_ref, x_ref, o_ref, cat_ref, *, n_heads):
    j = pl.program_id(1)
    aw = n_heads * ATTN_HEAD_DIM

    @pl.when(j == 0)
    def _():
        a = jnp.concatenate([attn_ref[0, hh].astype(F32) for hh in range(n_heads)], axis=1)
        cat_ref[:, 0:aw] = _rms(a, an_ref[...]).astype(BF16)
        cat_ref[:, aw:] = y_ref[...]

    o_ref[...] = x_ref[...] + jnp.dot(cat_ref[...], w_ref[...], preferred_element_type=F32)


def _proj_out(attn, y2, an, w_out, x2, *, seq):
    batch, n_heads, _, dh = attn.shape
    T, D = x2.shape
    aw = n_heads * dh
    sw = y2.shape[1]
    tm = _row_tile(seq, 512)
    tn = 512
    nsb = seq // tm
    return pl.pallas_call(
        functools.partial(_proj_out_kernel, n_heads=n_heads),
        grid=(T // tm, D // tn),
        in_specs=[
            pl.BlockSpec((1, n_heads, tm, dh), lambda i, j: (i // nsb, 0, i % nsb, 0)),
            pl.BlockSpec((tm, sw), lambda i, j: (i, 0)),
            pl.BlockSpec((1, aw), lambda i, j: (0, 0)),
            pl.BlockSpec((aw + sw, tn), lambda i, j: (0, j)),
            pl.BlockSpec((tm, tn), lambda i, j: (i, j)),
        ],
        out_specs=pl.BlockSpec((tm, tn), lambda i, j: (i, j)),
        out_shape=jax.ShapeDtypeStruct((T, D), F32),
        scratch_shapes=[pltpu.VMEM((tm, aw + sw), BF16)],
        compiler_params=_cparams("parallel", "arbitrary"),
        name="proj_out",
    )(attn, y2, an, w_out, x2)


def _ffn_kernel(x_ref, xh_ref, ln_ref, wg_ref, wv_ref, cwg_ref, cwv_ref, cbg_ref, cbv_ref,
                wd_ref, fn_ref, o_ref, h_ref, ug_ref, uv_ref, acc_ref, *, nsb, final):
    i = pl.program_id(0)
    j = pl.program_id(1)
    tm = x_ref.shape[0]
    hr = BF16_ROWS

    @pl.when(j == 0)
    def _():
        g = ln_ref[...]
        h_ref[hr:, :] = _rms(x_ref[...], g).astype(BF16)
        live = jnp.where(i % nsb != 0, 1.0, 0.0)
        h_ref[0:hr, :] = (_rms(xh_ref[...], g) * live).astype(BF16)
        acc_ref[...] = jnp.zeros_like(acc_ref)

    h = h_ref[...]
    ug_ref[...] = jnp.dot(h, wg_ref[...], preferred_element_type=F32)
    uv_ref[...] = jnp.dot(h, wv_ref[...], preferred_element_type=F32)

    def conv(u_ref, cw, cb):
        out = cb
        for t in range(FFN_CONV):
            o = hr - (FFN_CONV - 1) + t
            out = out + cw[t:t + 1, :] * u_ref[o:o + tm, :]
        return out

    cg = conv(ug_ref, cwg_ref[...], cbg_ref[...])
    cv = conv(uv_ref, cwv_ref[...], cbv_ref[...])
    act = (_silu(cg) * cv).astype(BF16)
    acc_ref[...] += jnp.dot(act, wd_ref[...], preferred_element_type=F32)

    @pl.when(j == pl.num_programs(1) - 1)
    def _():
        out = x_ref[...] + acc_ref[...]
        if final:
            out = _rms(out, fn_ref[...])
        o_ref[...] = out


def _ffn(x2, ln, w_up, conv_w, conv_b, w_down, fn, *, seq, final):
    T, D = x2.shape
    dff = w_down.shape[0]
    tm = _row_tile(seq, 512)
    tf = 512
    assert dff % tf == 0 and tm % BF16_ROWS == 0
    nsb = seq // tm
    nf = dff // tf
    hb = tm // BF16_ROWS
    return pl.pallas_call(
        functools.partial(_ffn_kernel, nsb=nsb, final=final),
        grid=(T // tm, nf),
        in_specs=[
            pl.BlockSpec((tm, D), lambda i, j: (i, 0)),
            pl.BlockSpec((BF16_ROWS, D), lambda i, j: (jnp.maximum(i * hb - 1, 0), 0)),
            pl.BlockSpec((1, D), lambda i, j: (0, 0)),
            pl.BlockSpec((D, tf), lambda i, j: (0, j)),
            pl.BlockSpec((D, tf), lambda i, j: (0, nf + j)),
            pl.BlockSpec((FFN_CONV, tf), lambda i, j: (0, j)),
            pl.BlockSpec((FFN_CONV, tf), lambda i, j: (0, nf + j)),
            pl.BlockSpec((1, tf), lambda i, j: (0, j)),
            pl.BlockSpec((1, tf), lambda i, j: (0, nf + j)),
            pl.BlockSpec((tf, D), lambda i, j: (j, 0)),
            pl.BlockSpec((1, D), lambda i, j: (0, 0)),
        ],
        out_specs=pl.BlockSpec((tm, D), lambda i, j: (i, 0)),
        out_shape=jax.ShapeDtypeStruct((T, D), F32),
        scratch_shapes=[
            pltpu.VMEM((tm + BF16_ROWS, D), BF16),
            pltpu.VMEM((tm + BF16_ROWS, tf), F32),
            pltpu.VMEM((tm + BF16_ROWS, tf), F32),
            pltpu.VMEM((tm, D), F32),
        ],
        compiler_params=_cparams("parallel", "arbitrary"),
        name="ffn_final" if final else "ffn",
    )(x2, x2, ln, w_up, w_up, conv_w, conv_w, conv_b, conv_b, w_down, fn)


def _rope_tables(seq):
    half = ATTN_HEAD_DIM // 2
    inv_freq = jnp.power(ROPE_THETA, -jnp.arange(half, dtype=F32) / half)
    ang = jnp.arange(seq, dtype=F32)[:, None] * inv_freq[None, :]
    cos, sin = jnp.cos(ang), jnp.sin(ang)
    return jnp.concatenate([cos, cos], axis=-1), jnp.concatenate([-sin, sin], axis=-1)


def _chunk_constants():
    L = SSD_CHUNK
    low = np.tril(np.ones((L, L), np.float32))
    tri = jnp.asarray(np.where(low > 0, 0.0, NEG_INF).astype(np.float32))
    ltri = jnp.asarray(low).astype(BF16)
    return tri, ltri


def _block_onehot(seq):
    oh = np.zeros((seq, LANES), np.float32)
    oh[np.arange(seq), np.arange(seq) // MOBA_BLOCK] = 1.0
    return jnp.asarray(oh).astype(BF16)


def _pad_lanes(v):
    return jnp.pad(v, (0, LANES - v.shape[0]))[None, :]


def kernel(x, ln1, w_in, attn_norm, ssm_conv_w, ssm_conv_b, dt_bias, a_log, d_skip, ssm_norm, w_out, ln2, w_up, ffn_conv_w, ffn_conv_b, w_down, final_norm):
    batch, seq, d_model = x.shape
    depth = ln1.shape[0]
    attn_w = attn_norm.shape[1]
    ssm_w = ssm_norm.shape[1]
    xbc_w = ssm_conv_w.shape[2]
    n_ssm_heads = a_log.shape[1]
    assert ssm_w // n_ssm_heads == SSM_HEAD_DIM and n_ssm_heads <= LANES
    main_w = 3 * attn_w + ssm_w + xbc_w
    cos_t, sin_t = _rope_tables(seq)
    tri, ltri = _chunk_constants()
    onehot = _block_onehot(seq)

    x2 = x.reshape(batch * seq, d_model)
    for i in range(depth):
        w_main = w_in[i][:, :main_w].astype(BF16)
        w_dt = jnp.pad(w_in[i][:, main_w:], ((0, 0), (0, LANES - n_ssm_heads))).astype(BF16)
        q, k, v, z2, xbc2, dt2 = _proj_in(
            x2, ln1[i][None, :], w_main, w_dt, cos_t, sin_t, ssm_conv_w[i], ssm_conv_b[i][None, :],
            batch=batch, seq=seq, attn_w=attn_w, ssm_w=ssm_w, xbc_w=xbc_w)

        attn = _moba(q, k, v, onehot)

        y3 = _ssd(
            xbc2.reshape(batch, seq, xbc_w), dt2.reshape(batch, seq, LANES),
            z2.reshape(batch, seq, ssm_w),
            _pad_lanes(dt_bias[i]), _pad_lanes(a_log[i]),
            jnp.repeat(d_skip[i], SSM_HEAD_DIM)[None, :], ssm_norm[i][None, :], tri, ltri,
            ssm_w=ssm_w)

        x2 = _proj_out(attn, y3.reshape(batch * seq, ssm_w), attn_norm[i][None, :],
                       w_out[i].astype(BF16), x2, seq=seq)

        x2 = _ffn(x2, ln2[i][None, :], w_up[i].astype(BF16), ffn_conv_w[i], ffn_conv_b[i][None, :],
                  w_down[i].astype(BF16), final_norm[None, :], seq=seq, final=(i == depth - 1))
    return x2.reshape(batch, seq, d_model)
```

```python
import functools

import numpy as np
import jax
import jax.numpy as jnp
from jax import lax
from jax.experimental import pallas as pl
from jax.experimental.pallas import tpu as pltpu

F32 = jnp.float32
BF16 = jnp.bfloat16

NORM_EPS = 1e-6
NEG_INF = -1e30
LOG2E = 1.4426950408889634
ROPE_THETA = 10000.0

ATTN_HEAD_DIM = 128
MOBA_BLOCK = 256
MOBA_TOPK = 3
MOBA_MAX_BLOCKS = 8

SSM_HEAD_DIM = 64
SSM_GROUPS = 8
SSM_STATE = 128
SSM_CONV = 4
SSD_CHUNK = 256
FFN_CONV = 3
CONV_ROW_CHUNK = 512

LANES = 128
BF16_ROWS = 16
VMEM_LIMIT_BYTES = 56 * 1024 * 1024


def _cparams(*sem):
    return pltpu.CompilerParams(dimension_semantics=sem, vmem_limit_bytes=VMEM_LIMIT_BYTES)


def _row_tile(seq, pref):
    t = min(seq, pref)
    assert seq % t == 0
    return t


def _rms(xf, g):
    ms = jnp.mean(xf * xf, axis=-1, keepdims=True)
    return xf * lax.rsqrt(ms + NORM_EPS) * g


def _silu(x):
    return x * (1.0 / (1.0 + jnp.exp(-x)))


def _nt_dot(a, b):
    return lax.dot_general(a, b, (((1,), (1,)), ((), ())), preferred_element_type=F32)


def _proj_in_kernel(x_ref, ln_ref, w_ref, wdt_ref, cos_ref, sin_ref, cw_ref, cb_ref,
                    q_ref, k_ref, v_ref, z_ref, xbc_ref, dt_ref,
                    h_ref, halo_ref, *, nq, nz, hpt, nsb):
    i = pl.program_id(0)
    j = pl.program_id(1)
    tm = x_ref.shape[0]

    @pl.when(j == 0)
    def _():
        h = _rms(x_ref[...], ln_ref[...]).astype(BF16)
        h_ref[...] = h
        dt_ref[...] = jnp.dot(h, wdt_ref[...], preferred_element_type=F32)

    def mm():
        return jnp.dot(h_ref[...], w_ref[...], preferred_element_type=F32)

    def rope_store(o_ref):
        acc = mm()
        cos = cos_ref[...]
        sin = sin_ref[...]
        for hh in range(hpt):
            a = acc[:, hh * LANES:(hh + 1) * LANES]
            o_ref[0, hh] = (a * cos + pltpu.roll(a, ATTN_HEAD_DIM // 2, axis=1) * sin).astype(BF16)

    @pl.when(j < nq)
    def _():
        rope_store(q_ref)

    @pl.when((j >= nq) & (j < 2 * nq))
    def _():
        rope_store(k_ref)

    @pl.when((j >= 2 * nq) & (j < 3 * nq))
    def _():
        acc = mm()
        for hh in range(hpt):
            v_ref[0, hh] = acc[:, hh * LANES:(hh + 1) * LANES].astype(BF16)

    @pl.when((j >= 3 * nq) & (j < 3 * nq + nz))
    def _():
        z_ref[...] = _silu(mm()).astype(BF16)

    @pl.when((i == 0) & (j == 0))
    def _():
        halo_ref[...] = jnp.zeros_like(halo_ref)

    @pl.when(j >= 3 * nq + nz)
    def _():
        jx = j - (3 * nq + nz)
        prev = jnp.where(i % nsb == 0, 0.0, halo_ref[jx])
        cw = cw_ref[...]
        row = lax.broadcasted_iota(jnp.int32, prev.shape, 0)
        rc = min(tm, CONV_ROW_CHUNK)
        for r0 in range(0, tm, rc):
            acc = jnp.dot(h_ref[r0:r0 + rc, :], w_ref[...], preferred_element_type=F32)
            conv = cb_ref[...] + cw[SSM_CONV - 1:SSM_CONV, :] * acc
            for sh in range(1, SSM_CONV):
                r = pltpu.roll(acc, sh, axis=0)
                head = jnp.where(row < sh, pltpu.roll(prev, sh, axis=0), r[0:8])
                xk = jnp.concatenate([head, r[8:]], axis=0)
                conv = conv + cw[SSM_CONV - 1 - sh:SSM_CONV - sh, :] * xk
            xbc_ref[r0:r0 + rc, :] = _silu(conv).astype(BF16)
            prev = acc[rc - 8:, :]
        halo_ref[jx] = prev


def _proj_in(x2, ln, w_main, w_dt, cos_t, sin_t, conv_w, conv_b, *, batch, seq, attn_w, ssm_w, xbc_w):
    T, D = x2.shape
    tm = _row_tile(seq, 1024)
    tn = 512
    nsb = seq // tm
    n_heads = attn_w // ATTN_HEAD_DIM
    hpt = tn // ATTN_HEAD_DIM
    nq = attn_w // tn
    nz = ssm_w // tn
    nx = xbc_w // tn
    nj = 3 * nq + nz + nx
    assert w_main.shape == (D, nj * tn)

    def clampj(lo, n):
        return lambda j: jnp.clip(j - lo, 0, n - 1)

    qj, kj, vj = clampj(0, nq), clampj(nq, nq), clampj(2 * nq, nq)
    zj, xj = clampj(3 * nq, nz), clampj(3 * nq + nz, nx)

    head_shape = jax.ShapeDtypeStruct((batch, n_heads, seq, ATTN_HEAD_DIM), BF16)

    def head_spec(fj):
        return pl.BlockSpec((1, hpt, tm, ATTN_HEAD_DIM), lambda i, j: (i // nsb, fj(j), i % nsb, 0))

    return pl.pallas_call(
        functools.partial(_proj_in_kernel, nq=nq, nz=nz, hpt=hpt, nsb=nsb),
        grid=(T // tm, nj),
        in_specs=[
            pl.BlockSpec((tm, D), lambda i, j: (i, 0)),
            pl.BlockSpec((1, D), lambda i, j: (0, 0)),
            pl.BlockSpec((D, tn), lambda i, j: (0, j)),
            pl.BlockSpec((D, LANES), lambda i, j: (0, 0)),
            pl.BlockSpec((tm, ATTN_HEAD_DIM), lambda i, j: (i % nsb, 0)),
            pl.BlockSpec((tm, ATTN_HEAD_DIM), lambda i, j: (i % nsb, 0)),
            pl.BlockSpec((SSM_CONV, tn), lambda i, j: (0, xj(j))),
            pl.BlockSpec((1, tn), lambda i, j: (0, xj(j))),
        ],
        out_specs=[
            head_spec(qj), head_spec(kj), head_spec(vj),
            pl.BlockSpec((tm, tn), lambda i, j: (i, zj(j))),
            pl.BlockSpec((tm, tn), lambda i, j: (i, xj(j))),
            pl.BlockSpec((tm, LANES), lambda i, j: (i, 0)),
        ],
        out_shape=[
            head_shape, head_shape, head_shape,
            jax.ShapeDtypeStruct((T, ssm_w), BF16),
            jax.ShapeDtypeStruct((T, xbc_w), BF16),
            jax.ShapeDtypeStruct((T, LANES), F32),
        ],
        scratch_shapes=[
            pltpu.VMEM((tm, D), BF16),
            pltpu.VMEM((nx, 8, tn), F32),
        ],
        compiler_params=_cparams("arbitrary", "arbitrary"),
        name="proj_in",
    )(x2, ln, w_main, w_dt, cos_t, sin_t, conv_w, conv_b)


def _moba_kernel(q_ref, k_ref, v_ref, oh_ref, o_ref, qa_ref, *, nb, scale):
    seq = nb * MOBA_BLOCK
    nbp = MOBA_MAX_BLOCKS
    blk = MOBA_BLOCK

    kf = k_ref[0, 0].astype(F32)
    rows = [jnp.sum(kf[b * blk:(b + 1) * blk], axis=0, keepdims=True) for b in range(nb)]
    if nb < nbp:
        rows.append(jnp.zeros((nbp - nb, ATTN_HEAD_DIM), F32))
    kmean = jnp.concatenate(rows, axis=0) * (1.0 / blk)
    k_hi = kmean.astype(BF16)
    k_lo = (kmean - k_hi.astype(F32)).astype(BF16)
    q = q_ref[0, 0]
    g2 = _nt_dot(jnp.concatenate([k_hi, k_lo], axis=0), q)
    gate = g2[0:nbp] + g2[nbp:2 * nbp]
    own = jnp.right_shift(lax.broadcasted_iota(jnp.int32, (nbp, seq), 1), blk.bit_length() - 1)
    kb = lax.broadcasted_iota(jnp.int32, (nbp, seq), 0)
    rank = jnp.zeros((nbp, seq), F32)
    for b in range(nb):
        gb = gate[b:b + 1, :]
        beats = (b < own) & ((gb > gate) | ((gb == gate) & (b < kb)))
        rank = rank + jnp.where(beats, 1.0, 0.0)
    allowed = ((kb < own) & (rank < MOBA_TOPK)) | (kb == own)
    bias_t = jnp.where(allowed, 0.0, NEG_INF)
    bias_t = jnp.concatenate([bias_t, jnp.zeros((LANES - nbp, seq), F32)], axis=0)
    qa_ref[:, 0:ATTN_HEAD_DIM] = q
    qa_ref[:, ATTN_HEAD_DIM:] = bias_t.T.astype(BF16)

    qi = lax.broadcasted_iota(jnp.int32, (blk, blk), 0)
    ki = lax.broadcasted_iota(jnp.int32, (blk, blk), 1)
    causal = ki <= qi
    c2 = scale * LOG2E
    for i in range(nb):
        hi = (i + 1) * blk
        qa = qa_ref[i * blk:hi, :]
        ka = jnp.concatenate([k_ref[0, 0, 0:hi, :], oh_ref[0:hi, :]], axis=1)
        s = _nt_dot(qa, ka)
        s_own = jnp.where(causal, s[:, i * blk:], NEG_INF)
        s = jnp.concatenate([s[:, 0:i * blk], s_own], axis=1) if i else s_own
        m = jnp.max(s, axis=1, keepdims=True)
        p = jnp.exp2((s - m) * c2)
        l = jnp.sum(p, axis=1, keepdims=True)
        acc = jnp.dot(p.astype(BF16), v_ref[0, 0, 0:hi, :], preferred_element_type=F32)
        o_ref[0, 0, i * blk:hi, :] = (acc / l).astype(BF16)


def _moba(q, k, v, onehot):
    batch, n_heads, seq, dh = q.shape
    nb = seq // MOBA_BLOCK
    assert seq % MOBA_BLOCK == 0 and nb <= MOBA_MAX_BLOCKS and dh == ATTN_HEAD_DIM
    full = pl.BlockSpec((1, 1, seq, dh), lambda b, h: (b, h, 0, 0))
    return pl.pallas_call(
        functools.partial(_moba_kernel, nb=nb, scale=dh ** -0.5),
        grid=(batch, n_heads),
        in_specs=[full, full, full, pl.BlockSpec((seq, LANES), lambda b, h: (0, 0))],
        out_specs=full,
        out_shape=jax.ShapeDtypeStruct((batch, n_heads, seq, dh), BF16),
        scratch_shapes=[pltpu.VMEM((seq, 2 * dh), BF16)],
        compiler_params=_cparams("parallel", "parallel"),
        name="moba",
    )(q, k, v, onehot)


def _ssd_kernel(xx_ref, xb_ref, xc_ref, dt_ref, dtb_ref, alog_ref, dskip_ref, z_ref, nw_ref,
                tri_ref, ltri_ref, y_ref,
                st_ref, acsc_ref, rows_ref, rowt_ref, *, hpg):
    c = pl.program_id(1)
    g = pl.program_id(2)
    L = SSD_CHUNK
    P = SSM_HEAD_DIM
    N = SSM_STATE
    gw = hpg * P
    n_groups = st_ref.shape[0]

    @pl.when(g == 0)
    def _():
        dtv = dt_ref[0] + dtb_ref[...]
        dtv = jnp.maximum(dtv, 0.0) + jnp.log1p(jnp.exp(-jnp.abs(dtv)))
        a = dtv * (-jnp.exp(alog_ref[...]))
        a_hi = a.astype(BF16)
        r1 = a - a_hi.astype(F32)
        a_mid = r1.astype(BF16)
        a_lo = (r1 - a_mid.astype(F32)).astype(BF16)
        cs3 = jnp.dot(ltri_ref[...], jnp.concatenate([a_hi, a_mid, a_lo], axis=1),
                      preferred_element_type=F32)
        acs2 = (cs3[:, 0:LANES] + cs3[:, LANES:2 * LANES] + cs3[:, 2 * LANES:]) * LOG2E
        rowt_ref[...] = (acs2 - jnp.log2(dtv)).T
        for gg in range(n_groups):
            sh = (LANES - hpg * gg) % LANES
            acsc_ref[gg] = pltpu.roll(acs2, sh, axis=1) if sh else acs2
            rows_ref[gg, 0:hpg, :] = rowt_ref[hpg * gg:hpg * (gg + 1), :]

    @pl.when(c == 0)
    def _():
        st_ref[g] = jnp.zeros((N, gw), F32)

    xs_b = xx_ref[0]
    bm_b = xb_ref[0]
    cm_b = xc_ref[0]
    cb_mat = _nt_dot(cm_b, bm_b)
    bm_t = bm_b.astype(F32).T
    acsc = acsc_ref[g]
    tri = tri_ref[...]
    lane_head = jnp.right_shift(lax.broadcasted_iota(jnp.int32, (1, gw), 1), P.bit_length() - 1)
    e_col = jnp.exp2(acsc)
    a_end = acsc[L - 1:L, :]

    mp, bw, xm = [], [], []
    dfs = jnp.zeros((L, gw), F32)
    dch = jnp.zeros((1, gw), F32)
    for r in range(hpg):
        a_col = acsc[:, r:r + 1]
        a_row = rows_ref[g, r:r + 1, :]
        a_last = a_end[:, r:r + 1]
        mp.append((cb_mat * jnp.exp2(a_col - a_row + tri)).astype(BF16))
        bw.append((bm_t * jnp.exp2(a_last - a_row)).astype(BF16))
        hmask = lane_head == r
        xm.append(jnp.where(hmask, xs_b, jnp.zeros_like(xs_b)))
        dfs = jnp.where(hmask, jnp.broadcast_to(e_col[:, r:r + 1], (L, gw)), dfs)
        dch = jnp.where(hmask, jnp.broadcast_to(jnp.exp2(a_last), (1, gw)), dch)
    mp = jnp.concatenate(mp, axis=1)
    bw = jnp.concatenate(bw, axis=1)
    xm = jnp.concatenate(xm, axis=0)

    st_old = st_ref[g]
    y = jnp.dot(mp, xm, preferred_element_type=F32)
    y = y + jnp.dot(cm_b, st_old.astype(BF16), preferred_element_type=F32) * dfs
    y = y + dskip_ref[...] * xs_b.astype(F32)
    st_ref[g] = st_old * dch + jnp.dot(bw, xm, preferred_element_type=F32)

    hg = y * z_ref[0].astype(F32)
    y_ref[0] = (_rms(hg, nw_ref[...])).astype(BF16)


def _ssd(xbc3, dt3, z3, dtb, alog, dskip, nw, tri, ltri, *, ssm_w):
    batch, seq, xbc_w = xbc3.shape
    L = SSD_CHUNK
    N = SSM_STATE
    G = SSM_GROUPS
    assert seq % L == 0
    nc = seq // L
    gw = ssm_w // G
    hpg = gw // SSM_HEAD_DIM
    assert hpg <= 8 and xbc_w == ssm_w + 2 * G * N and gw % LANES == 0
    b_off = ssm_w // N
    c_off = b_off + G

    return pl.pallas_call(
        functools.partial(_ssd_kernel, hpg=hpg),
        grid=(batch, nc, G),
        in_specs=[
            pl.BlockSpec((1, L, gw), lambda b, c, g: (b, c, g)),
            pl.BlockSpec((1, L, N), lambda b, c, g: (b, c, b_off + g)),
            pl.BlockSpec((1, L, N), lambda b, c, g: (b, c, c_off + g)),
            pl.BlockSpec((1, L, LANES), lambda b, c, g: (b, c, 0)),
            pl.BlockSpec((1, LANES), lambda b, c, g: (0, 0)),
            pl.BlockSpec((1, LANES), lambda b, c, g: (0, 0)),
            pl.BlockSpec((1, gw), lambda b, c, g: (0, g)),
            pl.BlockSpec((1, L, gw), lambda b, c, g: (b, c, g)),
            pl.BlockSpec((1, gw), lambda b, c, g: (0, g)),
            pl.BlockSpec((L, L), lambda b, c, g: (0, 0)),
            pl.BlockSpec((L, L), lambda b, c, g: (0, 0)),
        ],
        out_specs=pl.BlockSpec((1, L, gw), lambda b, c, g: (b, c, g)),
        out_shape=jax.ShapeDtypeStruct((batch, seq, ssm_w), BF16),
        scratch_shapes=[
            pltpu.VMEM((G, N, gw), F32),
            pltpu.VMEM((G, L, LANES), F32),
            pltpu.VMEM((G, 8, L), F32),
            pltpu.VMEM((LANES, L), F32),
        ],
        compiler_params=_cparams("parallel", "arbitrary", "arbitrary"),
        name="ssd",
    )(xbc3, xbc3, xbc3, dt3, dtb, alog, dskip, z3, nw, tri, ltri)


def _proj_out_kernel(attn_ref, y_ref, an_ref, w_ref, x_ref, o_ref, cat_ref, *, n_heads):
    j = pl.program_id(1)
    aw = n_heads * ATTN_HEAD_DIM

    @pl.when(j == 0)
    def _():
        a = jnp.concatenate([attn_ref[0, hh].astype(F32) for hh in range(n_heads)], axis=1)
        cat_ref[:, 0:aw] = _rms(a, an_ref[...]).astype(BF16)
        cat_ref[:, aw:] = y_ref[...]

    o_ref[...] = x_ref[...] + jnp.dot(cat_ref[...], w_ref[...], preferred_element_type=F32)


def _proj_out(attn, y2, an, w_out, x2, *, seq):
    batch, n_heads, _, dh = attn.shape
    T, D = x2.shape
    aw = n_heads * dh
    sw = y2.shape[1]
    tm = _row_tile(seq, 1024)
    tn = 512
    nsb = seq // tm
    return pl.pallas_call(
        functools.partial(_proj_out_kernel, n_heads=n_heads),
        grid=(T // tm, D // tn),
        in_specs=[
            pl.BlockSpec((1, n_heads, tm, dh), lambda i, j: (i // nsb, 0, i % nsb, 0)),
            pl.BlockSpec((tm, sw), lambda i, j: (i, 0)),
            pl.BlockSpec((1, aw), lambda i, j: (0, 0)),
            pl.BlockSpec((aw + sw, tn), lambda i, j: (0, j)),
            pl.BlockSpec((tm, tn), lambda i, j: (i, j)),
        ],
        out_specs=pl.BlockSpec((tm, tn), lambda i, j: (i, j)),
        out_shape=jax.ShapeDtypeStruct((T, D), F32),
        scratch_shapes=[pltpu.VMEM((tm, aw + sw), BF16)],
        compiler_params=_cparams("parallel", "arbitrary"),
        name="proj_out",
    )(attn, y2, an, w_out, x2)


def _ffn_kernel(x_ref, ln_ref, wu_ref, cw_ref, cb_ref, wd_ref, fn_ref, o_ref,
                h_ref, halo_ref, ug_ref, uv_ref, *, nsb, final):
    i = pl.program_id(0)
    j = pl.program_id(1)
    tm = x_ref.shape[0]
    tf = wd_ref.shape[0]

    @pl.when((i == 0) & (j == 0))
    def _():
        halo_ref[...] = jnp.zeros_like(halo_ref)

    @pl.when(j == 0)
    def _():
        x = x_ref[...]
        h_ref[...] = _rms(x, ln_ref[...]).astype(BF16)
        o_ref[...] = x

    first = i % nsb == 0
    h = h_ref[...]

    def up_conv(c0, u_ref):
        u = jnp.dot(h, wu_ref[:, c0:c0 + tf], preferred_element_type=F32)
        u_ref[0:8, :] = jnp.where(first, 0.0, halo_ref[j, :, c0:c0 + tf])
        u_ref[8:, :] = u
        halo_ref[j, :, c0:c0 + tf] = u[tm - 8:, :]
        out = cb_ref[:, c0:c0 + tf]
        for t in range(FFN_CONV):
            o = 8 - (FFN_CONV - 1) + t
            out = out + cw_ref[t:t + 1, c0:c0 + tf] * u_ref[o:o + tm, :]
        return out

    act = (_silu(up_conv(0, ug_ref)) * up_conv(tf, uv_ref)).astype(BF16)
    o_ref[...] += jnp.dot(act, wd_ref[...], preferred_element_type=F32)

    if final:
        @pl.when(j == pl.num_programs(1) - 1)
        def _():
            o_ref[...] = _rms(o_ref[...], fn_ref[...])


def _ffn(x2, ln, w_up_t, conv_w_t, conv_b_t, w_down, fn, *, seq, tf, final):
    T, D = x2.shape
    dff = w_down.shape[0]
    tm = _row_tile(seq, 512)
    assert dff % tf == 0 and w_up_t.shape == (D, 2 * dff)
    nsb = seq // tm
    nf = dff // tf
    return pl.pallas_call(
        functools.partial(_ffn_kernel, nsb=nsb, final=final),
        grid=(T // tm, nf),
        in_specs=[
            pl.BlockSpec((tm, D), lambda i, j: (i, 0)),
            pl.BlockSpec((1, D), lambda i, j: (0, 0)),
            pl.BlockSpec((D, 2 * tf), lambda i, j: (0, j)),
            pl.BlockSpec((FFN_CONV, 2 * tf), lambda i, j: (0, j)),
            pl.BlockSpec((1, 2 * tf), lambda i, j: (0, j)),
            pl.BlockSpec((tf, D), lambda i, j: (j, 0)),
            pl.BlockSpec((1, D), lambda i, j: (0, 0)),
        ],
        out_specs=pl.BlockSpec((tm, D), lambda i, j: (i, 0)),
        out_shape=jax.ShapeDtypeStruct((T, D), F32),
        scratch_shapes=[
            pltpu.VMEM((tm, D), BF16),
            pltpu.VMEM((nf, 8, 2 * tf), F32),
            pltpu.VMEM((tm + 8, tf), F32),
            pltpu.VMEM((tm + 8, tf), F32),
        ],
        compiler_params=_cparams("arbitrary", "arbitrary"),
        name="ffn_final" if final else "ffn",
    )(x2, ln, w_up_t, conv_w_t, conv_b_t, w_down, fn)


FFN_TILE = 512


def _gate_val_tiles(a, tf):
    lead = a.shape[:-1]
    dff = a.shape[-1] // 2
    t = a.reshape(lead + (2, dff // tf, tf))
    return jnp.swapaxes(t, -3, -2).reshape(lead + (2 * dff,))


def _rope_tables(seq):
    half = ATTN_HEAD_DIM // 2
    inv_freq = jnp.power(ROPE_THETA, -jnp.arange(half, dtype=F32) / half)
    ang = jnp.arange(seq, dtype=F32)[:, None] * inv_freq[None, :]
    cos, sin = jnp.cos(ang), jnp.sin(ang)
    return jnp.concatenate([cos, cos], axis=-1), jnp.concatenate([-sin, sin], axis=-1)


def _chunk_constants():
    L = SSD_CHUNK
    low = np.tril(np.ones((L, L), np.float32))
    tri = jnp.asarray(np.where(low > 0, 0.0, NEG_INF).astype(np.float32))
    ltri = jnp.asarray(low).astype(BF16)
    return tri, ltri


def _block_onehot(seq):
    oh = np.zeros((seq, LANES), np.float32)
    oh[np.arange(seq), np.arange(seq) // MOBA_BLOCK] = 1.0
    return jnp.asarray(oh).astype(BF16)


def _pad_lanes(v):
    return jnp.pad(v, (0, LANES - v.shape[0]))[None, :]


def kernel(x, ln1, w_in, attn_norm, ssm_conv_w, ssm_conv_b, dt_bias, a_log, d_skip, ssm_norm, w_out, ln2, w_up, ffn_conv_w, ffn_conv_b, w_down, final_norm):
    batch, seq, d_model = x.shape
    depth = ln1.shape[0]
    attn_w = attn_norm.shape[1]
    ssm_w = ssm_norm.shape[1]
    xbc_w = ssm_conv_w.shape[2]
    n_ssm_heads = a_log.shape[1]
    assert ssm_w // n_ssm_heads == SSM_HEAD_DIM and n_ssm_heads <= LANES
    main_w = 3 * attn_w + ssm_w + xbc_w
    cos_t, sin_t = _rope_tables(seq)
    tri, ltri = _chunk_constants()
    onehot = _block_onehot(seq)

    x2 = x.reshape(batch * seq, d_model)
    for i in range(depth):
        w_main = w_in[i][:, :main_w].astype(BF16)
        w_dt = jnp.pad(w_in[i][:, main_w:], ((0, 0), (0, LANES - n_ssm_heads))).astype(BF16)
        q, k, v, z2, xbc2, dt2 = _proj_in(
            x2, ln1[i][None, :], w_main, w_dt, cos_t, sin_t, ssm_conv_w[i], ssm_conv_b[i][None, :],
            batch=batch, seq=seq, attn_w=attn_w, ssm_w=ssm_w, xbc_w=xbc_w)

        attn = _moba(q, k, v, onehot)

        y3 = _ssd(
            xbc2.reshape(batch, seq, xbc_w), dt2.reshape(batch, seq, LANES),
            z2.reshape(batch, seq, ssm_w),
            _pad_lanes(dt_bias[i]), _pad_lanes(a_log[i]),
            jnp.repeat(d_skip[i], SSM_HEAD_DIM)[None, :], ssm_norm[i][None, :], tri, ltri,
            ssm_w=ssm_w)

        x2 = _proj_out(attn, y3.reshape(batch * seq, ssm_w), attn_norm[i][None, :],
                       w_out[i].astype(BF16), x2, seq=seq)

        x2 = _ffn(x2, ln2[i][None, :], _gate_val_tiles(w_up[i], FFN_TILE).astype(BF16),
                  _gate_val_tiles(ffn_conv_w[i], FFN_TILE), _gate_val_tiles(ffn_conv_b[i][None, :], FFN_TILE),
                  w_down[i].astype(BF16), final_norm[None, :], seq=seq, tf=FFN_TILE, final=(i == depth - 1))
    return x2.reshape(batch, seq, d_model)
```

```python
import functools

import numpy as np
import jax
import jax.numpy as jnp
from jax import lax
from jax.experimental import pallas as pl
from jax.experimental.pallas import tpu as pltpu

F32 = jnp.float32
BF16 = jnp.bfloat16

NORM_EPS = 1e-6
NEG_INF = -1e30
LOG2E = 1.4426950408889634
ROPE_THETA = 10000.0

ATTN_HEAD_DIM = 128
MOBA_BLOCK = 256
MOBA_TOPK = 3
MOBA_MAX_BLOCKS = 8

SSM_HEAD_DIM = 64
SSM_GROUPS = 8
SSM_STATE = 128
SSM_CONV = 4
SSD_CHUNK = 256
FFN_CONV = 3
CONV_ROW_CHUNK = 512

LANES = 128
BF16_ROWS = 16
VMEM_LIMIT_BYTES = 56 * 1024 * 1024


def _cparams(*sem):
    return pltpu.CompilerParams(dimension_semantics=sem, vmem_limit_bytes=VMEM_LIMIT_BYTES)


def _row_tile(seq, pref):
    t = min(seq, pref)
    assert seq % t == 0
    return t


def _rms(xf, g):
    ms = jnp.mean(xf * xf, axis=-1, keepdims=True)
    return xf * lax.rsqrt(ms + NORM_EPS) * g


def _silu(x):
    return x * (1.0 / (1.0 + jnp.exp(-x)))


def _nt_dot(a, b):
    return lax.dot_general(a, b, (((1,), (1,)), ((), ())), preferred_element_type=F32)


def _proj_in_kernel(x_ref, ln_ref, w_ref, wdt_ref, cos_ref, sin_ref, cw_ref, cb_ref,
                    q_ref, k_ref, v_ref, z_ref, xbc_ref, dt_ref,
                    h_ref, halo_ref, *, nq, nz, hpt, nsb):
    i = pl.program_id(0)
    j = pl.program_id(1)
    tm = x_ref.shape[0]

    @pl.when(j == 0)
    def _():
        h = _rms(x_ref[...], ln_ref[...]).astype(BF16)
        h_ref[...] = h
        dt_ref[...] = jnp.dot(h, wdt_ref[...], preferred_element_type=F32)

    def mm():
        return jnp.dot(h_ref[...], w_ref[...], preferred_element_type=F32)

    def rope_store(o_ref):
        acc = mm()
        cos = cos_ref[...]
        sin = sin_ref[...]
        for hh in range(hpt):
            a = acc[:, hh * LANES:(hh + 1) * LANES]
            o_ref[0, hh] = (a * cos + pltpu.roll(a, ATTN_HEAD_DIM // 2, axis=1) * sin).astype(BF16)

    @pl.when(j < nq)
    def _():
        rope_store(q_ref)

    @pl.when((j >= nq) & (j < 2 * nq))
    def _():
        rope_store(k_ref)

    @pl.when((j >= 2 * nq) & (j < 3 * nq))
    def _():
        acc = mm()
        for hh in range(hpt):
            v_ref[0, hh] = acc[:, hh * LANES:(hh + 1) * LANES].astype(BF16)

    @pl.when((j >= 3 * nq) & (j < 3 * nq + nz))
    def _():
        z_ref[...] = _silu(mm()).astype(BF16)

    @pl.when((i == 0) & (j == 0))
    def _():
        halo_ref[...] = jnp.zeros_like(halo_ref)

    @pl.when(j >= 3 * nq + nz)
    def _():
        jx = j - (3 * nq + nz)
        prev = jnp.where(i % nsb == 0, 0.0, halo_ref[jx])
        cw = cw_ref[...]
        row = lax.broadcasted_iota(jnp.int32, prev.shape, 0)
        rc = min(tm, CONV_ROW_CHUNK)
        for r0 in range(0, tm, rc):
            acc = jnp.dot(h_ref[r0:r0 + rc, :], w_ref[...], preferred_element_type=F32)
            conv = cb_ref[...] + cw[SSM_CONV - 1:SSM_CONV, :] * acc
            for sh in range(1, SSM_CONV):
                r = pltpu.roll(acc, sh, axis=0)
                head = jnp.where(row < sh, pltpu.roll(prev, sh, axis=0), r[0:8])
                xk = jnp.concatenate([head, r[8:]], axis=0)
                conv = conv + cw[SSM_CONV - 1 - sh:SSM_CONV - sh, :] * xk
            xbc_ref[r0:r0 + rc, :] = _silu(conv).astype(BF16)
            prev = acc[rc - 8:, :]
        halo_ref[jx] = prev


def _proj_in(x2, ln, w_main, w_dt, cos_t, sin_t, conv_w, conv_b, *, batch, seq, attn_w, ssm_w, xbc_w):
    T, D = x2.shape
    tm = _row_tile(seq, 1024)
    tn = 512
    nsb = seq // tm
    n_heads = attn_w // ATTN_HEAD_DIM
    hpt = tn // ATTN_HEAD_DIM
    nq = attn_w // tn
    nz = ssm_w // tn
    nx = xbc_w // tn
    nj = 3 * nq + nz + nx
    assert w_main.shape[0] == D and w_main.shape[1] >= nj * tn

    def clampj(lo, n):
        return lambda j: jnp.clip(j - lo, 0, n - 1)

    qj, kj, vj = clampj(0, nq), clampj(nq, nq), clampj(2 * nq, nq)
    zj, xj = clampj(3 * nq, nz), clampj(3 * nq + nz, nx)

    head_shape = jax.ShapeDtypeStruct((batch, n_heads, seq, ATTN_HEAD_DIM), BF16)

    def head_spec(fj):
        return pl.BlockSpec((1, hpt, tm, ATTN_HEAD_DIM), lambda i, j: (i // nsb, fj(j), i % nsb, 0))

    return pl.pallas_call(
        functools.partial(_proj_in_kernel, nq=nq, nz=nz, hpt=hpt, nsb=nsb),
        grid=(T // tm, nj),
        in_specs=[
            pl.BlockSpec((tm, D), lambda i, j: (i, 0)),
            pl.BlockSpec((1, D), lambda i, j: (0, 0)),
            pl.BlockSpec((D, tn), lambda i, j: (0, j)),
            pl.BlockSpec((D, LANES), lambda i, j: (0, 0)),
            pl.BlockSpec((tm, ATTN_HEAD_DIM), lambda i, j: (i % nsb, 0)),
            pl.BlockSpec((tm, ATTN_HEAD_DIM), lambda i, j: (i % nsb, 0)),
            pl.BlockSpec((SSM_CONV, tn), lambda i, j: (0, xj(j))),
            pl.BlockSpec((1, tn), lambda i, j: (0, xj(j))),
        ],
        out_specs=[
            head_spec(qj), head_spec(kj), head_spec(vj),
            pl.BlockSpec((tm, tn), lambda i, j: (i, zj(j))),
            pl.BlockSpec((tm, tn), lambda i, j: (i, xj(j))),
            pl.BlockSpec((tm, LANES), lambda i, j: (i, 0)),
        ],
        out_shape=[
            head_shape, head_shape, head_shape,
            jax.ShapeDtypeStruct((T, ssm_w), BF16),
            jax.ShapeDtypeStruct((T, xbc_w), BF16),
            jax.ShapeDtypeStruct((T, LANES), F32),
        ],
        scratch_shapes=[
            pltpu.VMEM((tm, D), BF16),
            pltpu.VMEM((nx, 8, tn), F32),
        ],
        compiler_params=_cparams("arbitrary", "arbitrary"),
        name="proj_in",
    )(x2, ln, w_main, w_dt, cos_t, sin_t, conv_w, conv_b)


def _moba_kernel(q_ref, k_ref, v_ref, oh_ref, o_ref, qa_ref, *, nb, scale):
    seq = nb * MOBA_BLOCK
    nbp = MOBA_MAX_BLOCKS
    blk = MOBA_BLOCK

    kf = k_ref[0, 0].astype(F32)
    rows = [jnp.sum(kf[b * blk:(b + 1) * blk], axis=0, keepdims=True) for b in range(nb)]
    if nb < nbp:
        rows.append(jnp.zeros((nbp - nb, ATTN_HEAD_DIM), F32))
    kmean = jnp.concatenate(rows, axis=0) * (1.0 / blk)
    k_hi = kmean.astype(BF16)
    k_lo = (kmean - k_hi.astype(F32)).astype(BF16)
    q = q_ref[0, 0]
    g2 = _nt_dot(jnp.concatenate([k_hi, k_lo], axis=0), q)
    gate = g2[0:nbp] + g2[nbp:2 * nbp]
    own = jnp.right_shift(lax.broadcasted_iota(jnp.int32, (nbp, seq), 1), blk.bit_length() - 1)
    kb = lax.broadcasted_iota(jnp.int32, (nbp, seq), 0)
    rank = jnp.zeros((nbp, seq), F32)
    for b in range(nb):
        gb = gate[b:b + 1, :]
        beats = (b < own) & ((gb > gate) | ((gb == gate) & (b < kb)))
        rank = rank + jnp.where(beats, 1.0, 0.0)
    allowed = ((kb < own) & (rank < MOBA_TOPK)) | (kb == own)
    bias_t = jnp.where(allowed, 0.0, NEG_INF)
    bias_t = jnp.concatenate([bias_t, jnp.zeros((LANES - nbp, seq), F32)], axis=0)
    qa_ref[:, 0:ATTN_HEAD_DIM] = q
    qa_ref[:, ATTN_HEAD_DIM:] = bias_t.T.astype(BF16)

    qi = lax.broadcasted_iota(jnp.int32, (blk, blk), 0)
    ki = lax.broadcasted_iota(jnp.int32, (blk, blk), 1)
    causal = ki <= qi
    c2 = scale * LOG2E
    for i in range(nb):
        hi = (i + 1) * blk
        qa = qa_ref[i * blk:hi, :]
        ka = jnp.concatenate([k_ref[0, 0, 0:hi, :], oh_ref[0:hi, :]], axis=1)
        s = _nt_dot(qa, ka)
        s_own = jnp.where(causal, s[:, i * blk:], NEG_INF)
        s = jnp.concatenate([s[:, 0:i * blk], s_own], axis=1) if i else s_own
        m = jnp.max(s, axis=1, keepdims=True)
        p = jnp.exp2((s - m) * c2)
        l = jnp.sum(p, axis=1, keepdims=True)
        acc = jnp.dot(p.astype(BF16), v_ref[0, 0, 0:hi, :], preferred_element_type=F32)
        o_ref[0, 0, i * blk:hi, :] = (acc / l).astype(BF16)


def _moba(q, k, v, onehot):
    batch, n_heads, seq, dh = q.shape
    nb = seq // MOBA_BLOCK
    assert seq % MOBA_BLOCK == 0 and nb <= MOBA_MAX_BLOCKS and dh == ATTN_HEAD_DIM
    full = pl.BlockSpec((1, 1, seq, dh), lambda b, h: (b, h, 0, 0))
    return pl.pallas_call(
        functools.partial(_moba_kernel, nb=nb, scale=dh ** -0.5),
        grid=(batch, n_heads),
        in_specs=[full, full, full, pl.BlockSpec((seq, LANES), lambda b, h: (0, 0))],
        out_specs=full,
        out_shape=jax.ShapeDtypeStruct((batch, n_heads, seq, dh), BF16),
        scratch_shapes=[pltpu.VMEM((seq, 2 * dh), BF16)],
        compiler_params=_cparams("parallel", "parallel"),
        name="moba",
    )(q, k, v, onehot)


def _ssd_kernel(xx_ref, xb_ref, xc_ref, dt_ref, dtb_ref, alog_ref, dskip_ref, z_ref, nw_ref,
                tri_ref, ltri_ref, y_ref,
                st_ref, acsc_ref, rows_ref, rowt_ref, *, hpg):
    c = pl.program_id(1)
    g = pl.program_id(2)
    L = SSD_CHUNK
    P = SSM_HEAD_DIM
    N = SSM_STATE
    gw = hpg * P
    n_groups = st_ref.shape[0]

    @pl.when(g == 0)
    def _():
        dtv = dt_ref[0] + dtb_ref[...]
        dtv = jnp.maximum(dtv, 0.0) + jnp.log1p(jnp.exp(-jnp.abs(dtv)))
        a = dtv * (-jnp.exp(alog_ref[...]))
        a_hi = a.astype(BF16)
        r1 = a - a_hi.astype(F32)
        a_mid = r1.astype(BF16)
        a_lo = (r1 - a_mid.astype(F32)).astype(BF16)
        cs3 = jnp.dot(ltri_ref[...], jnp.concatenate([a_hi, a_mid, a_lo], axis=1),
                      preferred_element_type=F32)
        acs2 = (cs3[:, 0:LANES] + cs3[:, LANES:2 * LANES] + cs3[:, 2 * LANES:]) * LOG2E
        rowt_ref[...] = (acs2 - jnp.log2(dtv)).T
        for gg in range(n_groups):
            sh = (LANES - hpg * gg) % LANES
            acsc_ref[gg] = pltpu.roll(acs2, sh, axis=1) if sh else acs2
            rows_ref[gg, 0:hpg, :] = rowt_ref[hpg * gg:hpg * (gg + 1), :]

    @pl.when(c == 0)
    def _():
        st_ref[g] = jnp.zeros((N, gw), F32)

    xs_b = xx_ref[0]
    bm_b = xb_ref[0]
    cm_b = xc_ref[0]
    cb_mat = _nt_dot(cm_b, bm_b)
    bm_t = bm_b.astype(F32).T
    acsc = acsc_ref[g]
    tri = tri_ref[...]
    lane_head = jnp.right_shift(lax.broadcasted_iota(jnp.int32, (1, gw), 1), P.bit_length() - 1)
    e_col = jnp.exp2(acsc)
    a_end = acsc[L - 1:L, :]

    mp, bw, xm = [], [], []
    dfs = jnp.zeros((L, gw), F32)
    dch = jnp.zeros((1, gw), F32)
    for r in range(hpg):
        a_col = acsc[:, r:r + 1]
        a_row = rows_ref[g, r:r + 1, :]
        a_last = a_end[:, r:r + 1]
        mp.append((cb_mat * jnp.exp2(a_col - a_row + tri)).astype(BF16))
        bw.append((bm_t * jnp.exp2(a_last - a_row)).astype(BF16))
        hmask = lane_head == r
        xm.append(jnp.where(hmask, xs_b, jnp.zeros_like(xs_b)))
        dfs = jnp.where(hmask, jnp.broadcast_to(e_col[:, r:r + 1], (L, gw)), dfs)
        dch = jnp.where(hmask, jnp.broadcast_to(jnp.exp2(a_last), (1, gw)), dch)
    mp = jnp.concatenate(mp, axis=1)
    bw = jnp.concatenate(bw, axis=1)
    xm = jnp.concatenate(xm, axis=0)

    st_old = st_ref[g]
    y = jnp.dot(mp, xm, preferred_element_type=F32)
    y = y + jnp.dot(cm_b, st_old.astype(BF16), preferred_element_type=F32) * dfs
    y = y + dskip_ref[...] * xs_b.astype(F32)
    st_ref[g] = st_old * dch + jnp.dot(bw, xm, preferred_element_type=F32)

    hg = y * z_ref[0].astype(F32)
    y_ref[0] = (_rms(hg, nw_ref[...])).astype(BF16)


def _ssd(xbc3, dt3, z3, dtb, alog, dskip, nw, tri, ltri, *, ssm_w):
    batch, seq, xbc_w = xbc3.shape
    L = SSD_CHUNK
    N = SSM_STATE
    G = SSM_GROUPS
    assert seq % L == 0
    nc = seq // L
    gw = ssm_w // G
    hpg = gw // SSM_HEAD_DIM
    assert hpg <= 8 and xbc_w == ssm_w + 2 * G * N and gw % LANES == 0
    b_off = ssm_w // N
    c_off = b_off + G

    return pl.pallas_call(
        functools.partial(_ssd_kernel, hpg=hpg),
        grid=(batch, nc, G),
        in_specs=[
            pl.BlockSpec((1, L, gw), lambda b, c, g: (b, c, g)),
            pl.BlockSpec((1, L, N), lambda b, c, g: (b, c, b_off + g)),
            pl.BlockSpec((1, L, N), lambda b, c, g: (b, c, c_off + g)),
            pl.BlockSpec((1, L, LANES), lambda b, c, g: (b, c, 0)),
            pl.BlockSpec((1, LANES), lambda b, c, g: (0, 0)),
            pl.BlockSpec((1, LANES), lambda b, c, g: (0, 0)),
            pl.BlockSpec((1, gw), lambda b, c, g: (0, g)),
            pl.BlockSpec((1, L, gw), lambda b, c, g: (b, c, g)),
            pl.BlockSpec((1, gw), lambda b, c, g: (0, g)),
            pl.BlockSpec((L, L), lambda b, c, g: (0, 0)),
            pl.BlockSpec((L, L), lambda b, c, g: (0, 0)),
        ],
        out_specs=pl.BlockSpec((1, L, gw), lambda b, c, g: (b, c, g)),
        out_shape=jax.ShapeDtypeStruct((batch, seq, ssm_w), BF16),
        scratch_shapes=[
            pltpu.VMEM((G, N, gw), F32),
            pltpu.VMEM((G, L, LANES), F32),
            pltpu.VMEM((G, 8, L), F32),
            pltpu.VMEM((LANES, L), F32),
        ],
        compiler_params=_cparams("parallel", "arbitrary", "arbitrary"),
        name="ssd",
    )(xbc3, xbc3, xbc3, dt3, dtb, alog, dskip, z3, nw, tri, ltri)


def _proj_out_kernel(attn_ref, y_ref, an_ref, w_ref, x_ref, o_ref, cat_ref, *, n_heads):
    j = pl.program_id(1)
    aw = n_heads * ATTN_HEAD_DIM

    @pl.when(j == 0)
    def _():
        a = jnp.concatenate([attn_ref[0, hh].astype(F32) for hh in range(n_heads)], axis=1)
        cat_ref[:, 0:aw] = _rms(a, an_ref[...]).astype(BF16)
        cat_ref[:, aw:] = y_ref[...]

    o_ref[...] = x_ref[...] + jnp.dot(cat_ref[...], w_ref[...], preferred_element_type=F32)


def _proj_out(attn, y2, an, w_out, x2, *, seq):
    batch, n_heads, _, dh = attn.shape
    T, D = x2.shape
    aw = n_heads * dh
    sw = y2.shape[1]
    tm = _row_tile(seq, 1024)
    tn = 512
    nsb = seq // tm
    return pl.pallas_call(
        functools.partial(_proj_out_kernel, n_heads=n_heads),
        grid=(T // tm, D // tn),
        in_specs=[
            pl.BlockSpec((1, n_heads, tm, dh), lambda i, j: (i // nsb, 0, i % nsb, 0)),
            pl.BlockSpec((tm, sw), lambda i, j: (i, 0)),
            pl.BlockSpec((1, aw), lambda i, j: (0, 0)),
            pl.BlockSpec((aw + sw, tn), lambda i, j: (0, j)),
            pl.BlockSpec((tm, tn), lambda i, j: (i, j)),
        ],
        out_specs=pl.BlockSpec((tm, tn), lambda i, j: (i, j)),
        out_shape=jax.ShapeDtypeStruct((T, D), F32),
        scratch_shapes=[pltpu.VMEM((tm, aw + sw), BF16)],
        compiler_params=_cparams("parallel", "arbitrary"),
        name="proj_out",
    )(attn, y2, an, w_out, x2)


def _ffn_kernel(x_ref, ln_ref, wg_ref, wv_ref, cwg_ref, cwv_ref, cbg_ref, cbv_ref, wd_ref, fn_ref, o_ref,
                h_ref, halo_ref, ug_ref, uv_ref, *, nsb, final):
    i = pl.program_id(0)
    j = pl.program_id(1)
    tm = x_ref.shape[0]

    @pl.when((i == 0) & (j == 0))
    def _():
        halo_ref[...] = jnp.zeros_like(halo_ref)

    @pl.when(j == 0)
    def _():
        x = x_ref[...]
        h_ref[...] = _rms(x, ln_ref[...]).astype(BF16)
        o_ref[...] = x

    first = i % nsb == 0
    h = h_ref[...]

    def up_conv(k, w_ref, cw_ref, cb_ref, u_ref):
        u = jnp.dot(h, w_ref[...], preferred_element_type=F32)
        u_ref[0:8, :] = jnp.where(first, 0.0, halo_ref[j, k])
        u_ref[8:, :] = u
        halo_ref[j, k] = u[tm - 8:, :]
        out = cb_ref[...]
        for t in range(FFN_CONV):
            o = 8 - (FFN_CONV - 1) + t
            out = out + cw_ref[t:t + 1, :] * u_ref[o:o + tm, :]
        return out

    act = (_silu(up_conv(0, wg_ref, cwg_ref, cbg_ref, ug_ref))
           * up_conv(1, wv_ref, cwv_ref, cbv_ref, uv_ref)).astype(BF16)
    o_ref[...] += jnp.dot(act, wd_ref[...], preferred_element_type=F32)

    if final:
        @pl.when(j == pl.num_programs(1) - 1)
        def _():
            o_ref[...] = _rms(o_ref[...], fn_ref[...])


def _ffn(x2, ln, w_up, conv_w, conv_b, w_down, fn, *, seq, final):
    T, D = x2.shape
    dff = w_down.shape[0]
    tm = _row_tile(seq, 512)
    tf = 512
    assert dff % tf == 0 and w_up.shape == (D, 2 * dff)
    nsb = seq // tm
    nf = dff // tf
    return pl.pallas_call(
        functools.partial(_ffn_kernel, nsb=nsb, final=final),
        grid=(T // tm, nf),
        in_specs=[
            pl.BlockSpec((tm, D), lambda i, j: (i, 0)),
            pl.BlockSpec((1, D), lambda i, j: (0, 0)),
            pl.BlockSpec((D, tf), lambda i, j: (0, j)),
            pl.BlockSpec((D, tf), lambda i, j: (0, nf + j)),
            pl.BlockSpec((FFN_CONV, tf), lambda i, j: (0, j)),
            pl.BlockSpec((FFN_CONV, tf), lambda i, j: (0, nf + j)),
            pl.BlockSpec((1, tf), lambda i, j: (0, j)),
            pl.BlockSpec((1, tf), lambda i, j: (0, nf + j)),
            pl.BlockSpec((tf, D), lambda i, j: (j, 0)),
            pl.BlockSpec((1, D), lambda i, j: (0, 0)),
        ],
        out_specs=pl.BlockSpec((tm, D), lambda i, j: (i, 0)),
        out_shape=jax.ShapeDtypeStruct((T, D), F32),
        scratch_shapes=[
            pltpu.VMEM((tm, D), BF16),
            pltpu.VMEM((nf, 2, 8, tf), F32),
            pltpu.VMEM((tm + 8, tf), F32),
            pltpu.VMEM((tm + 8, tf), F32),
        ],
        compiler_params=_cparams("arbitrary", "arbitrary"),
        name="ffn_final" if final else "ffn",
    )(x2, ln, w_up, w_up, conv_w, conv_w, conv_b, conv_b, w_down, fn)


def _rope_tables(seq):
    half = ATTN_HEAD_DIM // 2
    inv_freq = jnp.power(ROPE_THETA, -jnp.arange(half, dtype=F32) / half)
    ang = jnp.arange(seq, dtype=F32)[:, None] * inv_freq[None, :]
    cos, sin = jnp.cos(ang), jnp.sin(ang)
    return jnp.concatenate([cos, cos], axis=-1), jnp.concatenate([-sin, sin], axis=-1)


def _chunk_constants():
    L = SSD_CHUNK
    low = np.tril(np.ones((L, L), np.float32))
    tri = jnp.asarray(np.where(low > 0, 0.0, NEG_INF).astype(np.float32))
    ltri = jnp.asarray(low).astype(BF16)
    return tri, ltri


def _block_onehot(seq):
    oh = np.zeros((seq, LANES), np.float32)
    oh[np.arange(seq), np.arange(seq) // MOBA_BLOCK] = 1.0
    return jnp.asarray(oh).astype(BF16)


def _pad_lanes(v):
    return jnp.pad(v, (0, LANES - v.shape[0]))[None, :]


def kernel(x, ln1, w_in, attn_norm, ssm_conv_w, ssm_conv_b, dt_bias, a_log, d_skip, ssm_norm, w_out, ln2, w_up, ffn_conv_w, ffn_conv_b, w_down, final_norm):
    batch, seq, d_model = x.shape
    depth = ln1.shape[0]
    attn_w = attn_norm.shape[1]
    ssm_w = ssm_norm.shape[1]
    xbc_w = ssm_conv_w.shape[2]
    n_ssm_heads = a_log.shape[1]
    assert ssm_w // n_ssm_heads == SSM_HEAD_DIM and n_ssm_heads <= LANES
    main_w = 3 * attn_w + ssm_w + xbc_w
    cos_t, sin_t = _rope_tables(seq)
    tri, ltri = _chunk_constants()
    onehot = _block_onehot(seq)

    x2 = x.reshape(batch * seq, d_model)
    for i in range(depth):
        w_main = w_in[i].astype(BF16)
        w_dt = jnp.pad(w_main[:, main_w:], ((0, 0), (0, LANES - n_ssm_heads)))
        q, k, v, z2, xbc2, dt2 = _proj_in(
            x2, ln1[i][None, :], w_main, w_dt, cos_t, sin_t, ssm_conv_w[i], ssm_conv_b[i][None, :],
            batch=batch, seq=seq, attn_w=attn_w, ssm_w=ssm_w, xbc_w=xbc_w)

        attn = _moba(q, k, v, onehot)

        y3 = _ssd(
            xbc2.reshape(batch, seq, xbc_w), dt2.reshape(batch, seq, LANES),
            z2.reshape(batch, seq, ssm_w),
            _pad_lanes(dt_bias[i]), _pad_lanes(a_log[i]),
            jnp.repeat(d_skip[i], SSM_HEAD_DIM)[None, :], ssm_norm[i][None, :], tri, ltri,
            ssm_w=ssm_w)

        x2 = _proj_out(attn, y3.reshape(batch * seq, ssm_w), attn_norm[i][None, :],
                       w_out[i].astype(BF16), x2, seq=seq)

        x2 = _ffn(x2, ln2[i][None, :], w_up[i].astype(BF16), ffn_conv_w[i], ffn_conv_b[i][None, :],
                  w_down[i].astype(BF16), final_norm[None, :], seq=seq, final=(i == depth - 1))
    return x2.reshape(batch, seq, d_model)
```

```python
import functools

import numpy as np
import jax
import jax.numpy as jnp
from jax import lax
from jax.experimental import pallas as pl
from jax.experimental.pallas import tpu as pltpu

F32 = jnp.float32
BF16 = jnp.bfloat16

NORM_EPS = 1e-6
NEG_INF = -1e30
LOG2E = 1.4426950408889634
ROPE_THETA = 10000.0

ATTN_HEAD_DIM = 128
MOBA_BLOCK = 256
MOBA_TOPK = 3
MOBA_MAX_BLOCKS = 8
MOBA_LOOKAHEAD = 1

SSM_HEAD_DIM = 64
SSM_GROUPS = 8
SSM_STATE = 128
SSM_CONV = 4
SSD_CHUNK = 256
FFN_CONV = 3
ROW_CHUNK = 128

LANES = 128
BF16_ROWS = 16
VMEM_LIMIT_BYTES = 56 * 1024 * 1024


def _cparams(*sem):
    return pltpu.CompilerParams(dimension_semantics=sem, vmem_limit_bytes=VMEM_LIMIT_BYTES)


def _row_tile(seq, pref):
    t = min(seq, pref)
    assert seq % t == 0
    return t


def _rms(xf, g):
    ms = jnp.mean(xf * xf, axis=-1, keepdims=True)
    return xf * lax.rsqrt(ms + NORM_EPS) * g


def _silu(x):
    return x * (1.0 / (1.0 + jnp.exp(-x)))


def _nt_dot(a, b):
    return lax.dot_general(a, b, (((1,), (1,)), ((), ())), preferred_element_type=F32)


def _proj_in_kernel(x_ref, ln_ref, w_ref, wdt_ref, cos_ref, sin_ref, cw_ref, cb_ref,
                    q_ref, k_ref, v_ref, z_ref, xbc_ref, dt_ref,
                    h_ref, halo_ref, *, nq, nz, hpt, nsb):
    i = pl.program_id(0)
    j = pl.program_id(1)
    tm = x_ref.shape[0]

    @pl.when(j == 0)
    def _():
        h = _rms(x_ref[...], ln_ref[...]).astype(BF16)
        h_ref[...] = h
        dt_ref[...] = jnp.dot(h, wdt_ref[...], preferred_element_type=F32)

    rc = min(tm, ROW_CHUNK)
    starts = range(0, tm, rc)

    def mm_chunks():
        return [jnp.dot(h_ref[r0:r0 + rc, :], w_ref[...], preferred_element_type=F32) for r0 in starts]

    def rope_store(o_ref):
        accs = mm_chunks()
        for r0, acc in zip(starts, accs):
            cos = cos_ref[r0:r0 + rc, :]
            sin = sin_ref[r0:r0 + rc, :]
            for hh in range(hpt):
                a = acc[:, hh * LANES:(hh + 1) * LANES]
                o_ref[0, hh, r0:r0 + rc, :] = (
                    a * cos + pltpu.roll(a, ATTN_HEAD_DIM // 2, axis=1) * sin).astype(BF16)

    @pl.when(j < nq)
    def _():
        rope_store(q_ref)

    @pl.when((j >= nq) & (j < 2 * nq))
    def _():
        rope_store(k_ref)

    @pl.when((j >= 2 * nq) & (j < 3 * nq))
    def _():
        accs = mm_chunks()
        for r0, acc in zip(starts, accs):
            for hh in range(hpt):
                v_ref[0, hh, r0:r0 + rc, :] = acc[:, hh * LANES:(hh + 1) * LANES].astype(BF16)

    @pl.when((j >= 3 * nq) & (j < 3 * nq + nz))
    def _():
        accs = mm_chunks()
        for r0, acc in zip(starts, accs):
            z_ref[r0:r0 + rc, :] = _silu(acc).astype(BF16)

    @pl.when((i == 0) & (j == 0))
    def _():
        halo_ref[...] = jnp.zeros_like(halo_ref)

    @pl.when(j >= 3 * nq + nz)
    def _():
        jx = j - (3 * nq + nz)
        prev = jnp.where(i % nsb == 0, 0.0, halo_ref[jx])
        cw = cw_ref[...]
        row = lax.broadcasted_iota(jnp.int32, prev.shape, 0)
        accs = mm_chunks()
        for r0, acc in zip(starts, accs):
            conv = cb_ref[...] + cw[SSM_CONV - 1:SSM_CONV, :] * acc
            for sh in range(1, SSM_CONV):
                r = pltpu.roll(acc, sh, axis=0)
                head = jnp.where(row < sh, pltpu.roll(prev, sh, axis=0), r[0:8])
                xk = jnp.concatenate([head, r[8:]], axis=0)
                conv = conv + cw[SSM_CONV - 1 - sh:SSM_CONV - sh, :] * xk
            xbc_ref[r0:r0 + rc, :] = _silu(conv).astype(BF16)
            prev = acc[rc - 8:, :]
        halo_ref[jx] = prev


def _proj_in(x2, ln, w_main, w_dt, cos_t, sin_t, conv_w, conv_b, *, batch, seq, attn_w, ssm_w, xbc_w):
    T, D = x2.shape
    tm = _row_tile(seq, 1024)
    tn = 512
    nsb = seq // tm
    n_heads = attn_w // ATTN_HEAD_DIM
    hpt = tn // ATTN_HEAD_DIM
    nq = attn_w // tn
    nz = ssm_w // tn
    nx = xbc_w // tn
    nj = 3 * nq + nz + nx
    assert w_main.shape[0] == D and w_main.shape[1] >= nj * tn

    def clampj(lo, n):
        return lambda j: jnp.clip(j - lo, 0, n - 1)

    qj, kj, vj = clampj(0, nq), clampj(nq, nq), clampj(2 * nq, nq)
    zj, xj = clampj(3 * nq, nz), clampj(3 * nq + nz, nx)

    head_shape = jax.ShapeDtypeStruct((batch, n_heads, seq, ATTN_HEAD_DIM), BF16)

    def head_spec(fj):
        return pl.BlockSpec((1, hpt, tm, ATTN_HEAD_DIM), lambda i, j: (i // nsb, fj(j), i % nsb, 0))

    return pl.pallas_call(
        functools.partial(_proj_in_kernel, nq=nq, nz=nz, hpt=hpt, nsb=nsb),
        grid=(T // tm, nj),
        in_specs=[
            pl.BlockSpec((tm, D), lambda i, j: (i, 0)),
            pl.BlockSpec((1, D), lambda i, j: (0, 0)),
            pl.BlockSpec((D, tn), lambda i, j: (0, j)),
            pl.BlockSpec((D, LANES), lambda i, j: (0, 0)),
            pl.BlockSpec((tm, ATTN_HEAD_DIM), lambda i, j: (i % nsb, 0)),
            pl.BlockSpec((tm, ATTN_HEAD_DIM), lambda i, j: (i % nsb, 0)),
            pl.BlockSpec((SSM_CONV, tn), lambda i, j: (0, xj(j))),
            pl.BlockSpec((1, tn), lambda i, j: (0, xj(j))),
        ],
        out_specs=[
            head_spec(qj), head_spec(kj), head_spec(vj),
            pl.BlockSpec((tm, tn), lambda i, j: (i, zj(j))),
            pl.BlockSpec((tm, tn), lambda i, j: (i, xj(j))),
            pl.BlockSpec((tm, LANES), lambda i, j: (i, 0)),
        ],
        out_shape=[
            head_shape, head_shape, head_shape,
            jax.ShapeDtypeStruct((T, ssm_w), BF16),
            jax.ShapeDtypeStruct((T, xbc_w), BF16),
            jax.ShapeDtypeStruct((T, LANES), F32),
        ],
        scratch_shapes=[
            pltpu.VMEM((tm, D), BF16),
            pltpu.VMEM((nx, 8, tn), F32),
        ],
        compiler_params=_cparams("arbitrary", "arbitrary"),
        name="proj_in",
    )(x2, ln, w_main, w_dt, cos_t, sin_t, conv_w, conv_b)


def _moba_kernel(q_ref, k_ref, v_ref, oh_ref, o_ref, qa_ref, *, nb, scale):
    seq = nb * MOBA_BLOCK
    nbp = MOBA_MAX_BLOCKS
    blk = MOBA_BLOCK

    kf = k_ref[0, 0].astype(F32)
    rows = [jnp.sum(kf[b * blk:(b + 1) * blk], axis=0, keepdims=True) for b in range(nb)]
    if nb < nbp:
        rows.append(jnp.zeros((nbp - nb, ATTN_HEAD_DIM), F32))
    kmean = jnp.concatenate(rows, axis=0) * (1.0 / blk)
    k_hi = kmean.astype(BF16)
    k_lo = (kmean - k_hi.astype(F32)).astype(BF16)
    q = q_ref[0, 0]
    g2 = _nt_dot(jnp.concatenate([k_hi, k_lo], axis=0), q)
    gate = g2[0:nbp] + g2[nbp:2 * nbp]
    own = jnp.right_shift(lax.broadcasted_iota(jnp.int32, (nbp, seq), 1), blk.bit_length() - 1)
    kb = lax.broadcasted_iota(jnp.int32, (nbp, seq), 0)
    rank = jnp.zeros((nbp, seq), F32)
    for b in range(nb):
        gb = gate[b:b + 1, :]
        beats = (b < own) & ((gb > gate) | ((gb == gate) & (b < kb)))
        rank = rank + jnp.where(beats, 1.0, 0.0)
    allowed = ((kb < own) & (rank < MOBA_TOPK)) | (kb == own)
    bias_t = jnp.where(allowed, 0.0, NEG_INF)
    bias_t = jnp.concatenate([bias_t, jnp.zeros((LANES - nbp, seq), F32)], axis=0)
    qa_ref[:, 0:ATTN_HEAD_DIM] = q
    qa_ref[:, ATTN_HEAD_DIM:] = bias_t.T.astype(BF16)

    qi = lax.broadcasted_iota(jnp.int32, (blk, blk), 0)
    ki = lax.broadcasted_iota(jnp.int32, (blk, blk), 1)
    causal = ki <= qi
    c2 = scale * LOG2E
    def scores(i):
        hi = (i + 1) * blk
        ka = jnp.concatenate([k_ref[0, 0, 0:hi, :], oh_ref[0:hi, :]], axis=1)
        return _nt_dot(qa_ref[i * blk:hi, :], ka)

    ahead = [scores(i) for i in range(min(MOBA_LOOKAHEAD, nb))]
    for i in range(nb):
        hi = (i + 1) * blk
        s = ahead.pop(0)
        if i + MOBA_LOOKAHEAD < nb:
            ahead.append(scores(i + MOBA_LOOKAHEAD))
        s_own = jnp.where(causal, s[:, i * blk:], NEG_INF)
        s = jnp.concatenate([s[:, 0:i * blk], s_own], axis=1) if i else s_own
        m = jnp.max(s, axis=1, keepdims=True)
        p = jnp.exp2((s - m) * c2)
        l = jnp.sum(p, axis=1, keepdims=True)
        acc = jnp.dot(p.astype(BF16), v_ref[0, 0, 0:hi, :], preferred_element_type=F32)
        o_ref[0, 0, i * blk:hi, :] = (acc / l).astype(BF16)


def _moba(q, k, v, onehot):
    batch, n_heads, seq, dh = q.shape
    nb = seq // MOBA_BLOCK
    assert seq % MOBA_BLOCK == 0 and nb <= MOBA_MAX_BLOCKS and dh == ATTN_HEAD_DIM
    full = pl.BlockSpec((1, 1, seq, dh), lambda b, h: (b, h, 0, 0))
    return pl.pallas_call(
        functools.partial(_moba_kernel, nb=nb, scale=dh ** -0.5),
        grid=(batch, n_heads),
        in_specs=[full, full, full, pl.BlockSpec((seq, LANES), lambda b, h: (0, 0))],
        out_specs=full,
        out_shape=jax.ShapeDtypeStruct((batch, n_heads, seq, dh), BF16),
        scratch_shapes=[pltpu.VMEM((seq, 2 * dh), BF16)],
        compiler_params=_cparams("parallel", "parallel"),
        name="moba",
    )(q, k, v, onehot)


def _ssd_kernel(xx_ref, xb_ref, xc_ref, dt_ref, dtb_ref, alog_ref, dskip_ref, z_ref, nw_ref,
                tri_ref, ltri_ref, y_ref,
                st_ref, acsc_ref, rows_ref, rowt_ref, *, hpg):
    c = pl.program_id(1)
    g = pl.program_id(2)
    L = SSD_CHUNK
    P = SSM_HEAD_DIM
    N = SSM_STATE
    gw = hpg * P
    n_groups = st_ref.shape[0]

    @pl.when(g == 0)
    def _():
        dtv = dt_ref[0] + dtb_ref[...]
        dtv = jnp.maximum(dtv, 0.0) + jnp.log1p(jnp.exp(-jnp.abs(dtv)))
        a = dtv * (-jnp.exp(alog_ref[...]))
        a_hi = a.astype(BF16)
        r1 = a - a_hi.astype(F32)
        a_mid = r1.astype(BF16)
        a_lo = (r1 - a_mid.astype(F32)).astype(BF16)
        cs3 = jnp.dot(ltri_ref[...], jnp.concatenate([a_hi, a_mid, a_lo], axis=1),
                      preferred_element_type=F32)
        acs2 = (cs3[:, 0:LANES] + cs3[:, LANES:2 * LANES] + cs3[:, 2 * LANES:]) * LOG2E
        rowt_ref[...] = (acs2 - jnp.log2(dtv)).T
        for gg in range(n_groups):
            sh = (LANES - hpg * gg) % LANES
            acsc_ref[gg] = pltpu.roll(acs2, sh, axis=1) if sh else acs2
            rows_ref[gg, 0:hpg, :] = rowt_ref[hpg * gg:hpg * (gg + 1), :]

    @pl.when(c == 0)
    def _():
        st_ref[g] = jnp.zeros((N, gw), F32)

    xs_b = xx_ref[0]
    bm_b = xb_ref[0]
    cm_b = xc_ref[0]
    cb_mat = _nt_dot(cm_b, bm_b)
    bm_t = bm_b.astype(F32).T
    acsc = acsc_ref[g]
    tri = tri_ref[...]
    lane_head = jnp.right_shift(lax.broadcasted_iota(jnp.int32, (1, gw), 1), P.bit_length() - 1)
    e_col = jnp.exp2(acsc)
    a_end = acsc[L - 1:L, :]

    mp, bw, xm = [], [], []
    dfs = jnp.zeros((L, gw), F32)
    dch = jnp.zeros((1, gw), F32)
    for r in range(hpg):
        a_col = acsc[:, r:r + 1]
        a_row = rows_ref[g, r:r + 1, :]
        a_last = a_end[:, r:r + 1]
        mp.append((cb_mat * jnp.exp2(a_col - a_row + tri)).astype(BF16))
        bw.append((bm_t * jnp.exp2(a_last - a_row)).astype(BF16))
        hmask = lane_head == r
        xm.append(jnp.where(hmask, xs_b, jnp.zeros_like(xs_b)))
        dfs = jnp.where(hmask, jnp.broadcast_to(e_col[:, r:r + 1], (L, gw)), dfs)
        dch = jnp.where(hmask, jnp.broadcast_to(jnp.exp2(a_last), (1, gw)), dch)
    mp = jnp.concatenate(mp, axis=1)
    bw = jnp.concatenate(bw, axis=1)
    xm = jnp.concatenate(xm, axis=0)

    st_old = st_ref[g]
    y = jnp.dot(mp, xm, preferred_element_type=F32)
    y = y + jnp.dot(cm_b, st_old.astype(BF16), preferred_element_type=F32) * dfs
    y = y + dskip_ref[...] * xs_b.astype(F32)
    st_ref[g] = st_old * dch + jnp.dot(bw, xm, preferred_element_type=F32)

    hg = y * z_ref[0].astype(F32)
    y_ref[0] = (_rms(hg, nw_ref[...])).astype(BF16)


def _ssd(xbc3, dt3, z3, dtb, alog, dskip, nw, tri, ltri, *, ssm_w):
    batch, seq, xbc_w = xbc3.shape
    L = SSD_CHUNK
    N = SSM_STATE
    G = SSM_GROUPS
    assert seq % L == 0
    nc = seq // L
    gw = ssm_w // G
    hpg = gw // SSM_HEAD_DIM
    assert hpg <= 8 and xbc_w == ssm_w + 2 * G * N and gw % LANES == 0
    b_off = ssm_w // N
    c_off = b_off + G

    return pl.pallas_call(
        functools.partial(_ssd_kernel, hpg=hpg),
        grid=(batch, nc, G),
        in_specs=[
            pl.BlockSpec((1, L, gw), lambda b, c, g: (b, c, g)),
            pl.BlockSpec((1, L, N), lambda b, c, g: (b, c, b_off + g)),
            pl.BlockSpec((1, L, N), lambda b, c, g: (b, c, c_off + g)),
            pl.BlockSpec((1, L, LANES), lambda b, c, g: (b, c, 0)),
            pl.BlockSpec((1, LANES), lambda b, c, g: (0, 0)),
            pl.BlockSpec((1, LANES), lambda b, c, g: (0, 0)),
            pl.BlockSpec((1, gw), lambda b, c, g: (0, g)),
            pl.BlockSpec((1, L, gw), lambda b, c, g: (b, c, g)),
            pl.BlockSpec((1, gw), lambda b, c, g: (0, g)),
            pl.BlockSpec((L, L), lambda b, c, g: (0, 0)),
            pl.BlockSpec((L, L), lambda b, c, g: (0, 0)),
        ],
        out_specs=pl.BlockSpec((1, L, gw), lambda b, c, g: (b, c, g)),
        out_shape=jax.ShapeDtypeStruct((batch, seq, ssm_w), BF16),
        scratch_shapes=[
            pltpu.VMEM((G, N, gw), F32),
            pltpu.VMEM((G, L, LANES), F32),
            pltpu.VMEM((G, 8, L), F32),
            pltpu.VMEM((LANES, L), F32),
        ],
        compiler_params=_cparams("parallel", "arbitrary", "arbitrary"),
        name="ssd",
    )(xbc3, xbc3, xbc3, dt3, dtb, alog, dskip, z3, nw, tri, ltri)


def _proj_out_kernel(attn_ref, y_ref, an_ref, w_ref, x_ref, o_ref, cat_ref, *, n_heads):
    j = pl.program_id(1)
    aw = n_heads * ATTN_HEAD_DIM

    @pl.when(j == 0)
    def _():
        a = jnp.concatenate([attn_ref[0, hh].astype(F32) for hh in range(n_heads)], axis=1)
        cat_ref[:, 0:aw] = _rms(a, an_ref[...]).astype(BF16)
        cat_ref[:, aw:] = y_ref[...]

    tm = x_ref.shape[0]
    rc = min(tm, ROW_CHUNK)
    starts = range(0, tm, rc)
    accs = [jnp.dot(cat_ref[r0:r0 + rc, :], w_ref[...], preferred_element_type=F32) for r0 in starts]
    for r0, acc in zip(starts, accs):
        o_ref[r0:r0 + rc, :] = x_ref[r0:r0 + rc, :] + acc


def _proj_out(attn, y2, an, w_out, x2, *, seq):
    batch, n_heads, _, dh = attn.shape
    T, D = x2.shape
    aw = n_heads * dh
    sw = y2.shape[1]
    tm = _row_tile(seq, 1024)
    tn = 512
    nsb = seq // tm
    return pl.pallas_call(
        functools.partial(_proj_out_kernel, n_heads=n_heads),
        grid=(T // tm, D // tn),
        in_specs=[
            pl.BlockSpec((1, n_heads, tm, dh), lambda i, j: (i // nsb, 0, i % nsb, 0)),
            pl.BlockSpec((tm, sw), lambda i, j: (i, 0)),
            pl.BlockSpec((1, aw), lambda i, j: (0, 0)),
            pl.BlockSpec((aw + sw, tn), lambda i, j: (0, j)),
            pl.BlockSpec((tm, tn), lambda i, j: (i, j)),
        ],
        out_specs=pl.BlockSpec((tm, tn), lambda i, j: (i, j)),
        out_shape=jax.ShapeDtypeStruct((T, D), F32),
        scratch_shapes=[pltpu.VMEM((tm, aw + sw), BF16)],
        compiler_params=_cparams("parallel", "arbitrary"),
        name="proj_out",
    )(attn, y2, an, w_out, x2)


def _ffn_kernel(x_ref, ln_ref, wg_ref, wv_ref, cwg_ref, cwv_ref, cbg_ref, cbv_ref, wd_ref, fn_ref, o_ref,
                h_ref, halo_ref, ug_ref, uv_ref, *, nsb, final):
    i = pl.program_id(0)
    j = pl.program_id(1)
    tm = x_ref.shape[0]

    @pl.when((i == 0) & (j == 0))
    def _():
        halo_ref[...] = jnp.zeros_like(halo_ref)

    @pl.when(j == 0)
    def _():
        x = x_ref[...]
        h_ref[...] = _rms(x, ln_ref[...]).astype(BF16)
        o_ref[...] = x

    first = i % nsb == 0
    rc = min(tm, ROW_CHUNK)
    starts = range(0, tm, rc)
    ups = [(jnp.dot(h_ref[r0:r0 + rc, :], wg_ref[...], preferred_element_type=F32),
            jnp.dot(h_ref[r0:r0 + rc, :], wv_ref[...], preferred_element_type=F32)) for r0 in starts]
    ug_ref[0:8, :] = jnp.where(first, 0.0, halo_ref[j, 0])
    uv_ref[0:8, :] = jnp.where(first, 0.0, halo_ref[j, 1])
    halo_ref[j, 0] = ups[-1][0][rc - 8:, :]
    halo_ref[j, 1] = ups[-1][1][rc - 8:, :]

    def conv(u_ref, cw_ref, cb_ref, r0):
        out = cb_ref[...]
        for t in range(FFN_CONV):
            o = 8 - (FFN_CONV - 1) + t + r0
            out = out + cw_ref[t:t + 1, :] * u_ref[o:o + rc, :]
        return out

    for r0, (g, v) in zip(starts, ups):
        ug_ref[8 + r0:8 + r0 + rc, :] = g
        uv_ref[8 + r0:8 + r0 + rc, :] = v
        act = (_silu(conv(ug_ref, cwg_ref, cbg_ref, r0)) * conv(uv_ref, cwv_ref, cbv_ref, r0)).astype(BF16)
        o_ref[r0:r0 + rc, :] += jnp.dot(act, wd_ref[...], preferred_element_type=F32)

    if final:
        @pl.when(j == pl.num_programs(1) - 1)
        def _():
            o_ref[...] = _rms(o_ref[...], fn_ref[...])


def _ffn(x2, ln, w_up, conv_w, conv_b, w_down, fn, *, seq, final):
    T, D = x2.shape
    dff = w_down.shape[0]
    tm = _row_tile(seq, 512)
    tf = 512
    assert dff % tf == 0 and w_up.shape == (D, 2 * dff)
    nsb = seq // tm
    nf = dff // tf
    return pl.pallas_call(
        functools.partial(_ffn_kernel, nsb=nsb, final=final),
        grid=(T // tm, nf),
        in_specs=[
            pl.BlockSpec((tm, D), lambda i, j: (i, 0)),
            pl.BlockSpec((1, D), lambda i, j: (0, 0)),
            pl.BlockSpec((D, tf), lambda i, j: (0, j)),
            pl.BlockSpec((D, tf), lambda i, j: (0, nf + j)),
            pl.BlockSpec((FFN_CONV, tf), lambda i, j: (0, j)),
            pl.BlockSpec((FFN_CONV, tf), lambda i, j: (0, nf + j)),
            pl.BlockSpec((1, tf), lambda i, j: (0, j)),
            pl.BlockSpec((1, tf), lambda i, j: (0, nf + j)),
            pl.BlockSpec((tf, D), lambda i, j: (j, 0)),
            pl.BlockSpec((1, D), lambda i, j: (0, 0)),
        ],
        out_specs=pl.BlockSpec((tm, D), lambda i, j: (i, 0)),
        out_shape=jax.ShapeDtypeStruct((T, D), F32),
        scratch_shapes=[
            pltpu.VMEM((tm, D), BF16),
            pltpu.VMEM((nf, 2, 8, tf), F32),
            pltpu.VMEM((tm + 8, tf), F32),
            pltpu.VMEM((tm + 8, tf), F32),
        ],
        compiler_params=_cparams("arbitrary", "arbitrary"),
        name="ffn_final" if final else "ffn",
    )(x2, ln, w_up, w_up, conv_w, conv_w, conv_b, conv_b, w_down, fn)


def _rope_tables(seq):
    half = ATTN_HEAD_DIM // 2
    inv_freq = jnp.power(ROPE_THETA, -jnp.arange(half, dtype=F32) / half)
    ang = jnp.arange(seq, dtype=F32)[:, None] * inv_freq[None, :]
    cos, sin = jnp.cos(ang), jnp.sin(ang)
    return jnp.concatenate([cos, cos], axis=-1), jnp.concatenate([-sin, sin], axis=-1)


def _chunk_constants():
    L = SSD_CHUNK
    low = np.tril(np.ones((L, L), np.float32))
    tri = jnp.asarray(np.where(low > 0, 0.0, NEG_INF).astype(np.float32))
    ltri = jnp.asarray(low).astype(BF16)
    return tri, ltri


def _block_onehot(seq):
    oh = np.zeros((seq, LANES), np.float32)
    oh[np.arange(seq), np.arange(seq) // MOBA_BLOCK] = 1.0
    return jnp.asarray(oh).astype(BF16)


def _pad_lanes(v):
    return jnp.pad(v, (0, LANES - v.shape[0]))[None, :]


def kernel(x, ln1, w_in, attn_norm, ssm_conv_w, ssm_conv_b, dt_bias, a_log, d_skip, ssm_norm, w_out, ln2, w_up, ffn_conv_w, ffn_conv_b, w_down, final_norm):
    batch, seq, d_model = x.shape
    depth = ln1.shape[0]
    attn_w = attn_norm.shape[1]
    ssm_w = ssm_norm.shape[1]
    xbc_w = ssm_conv_w.shape[2]
    n_ssm_heads = a_log.shape[1]
    assert ssm_w // n_ssm_heads == SSM_HEAD_DIM and n_ssm_heads <= LANES
    main_w = 3 * attn_w + ssm_w + xbc_w
    cos_t, sin_t = _rope_tables(seq)
    tri, ltri = _chunk_constants()
    onehot = _block_onehot(seq)

    x2 = x.reshape(batch * seq, d_model)
    for i in range(depth):
        w_main = w_in[i].astype(BF16)
        w_dt = jnp.pad(w_main[:, main_w:], ((0, 0), (0, LANES - n_ssm_heads)))
        q, k, v, z2, xbc2, dt2 = _proj_in(
            x2, ln1[i][None, :], w_main, w_dt, cos_t, sin_t, ssm_conv_w[i], ssm_conv_b[i][None, :],
            batch=batch, seq=seq, attn_w=attn_w, ssm_w=ssm_w, xbc_w=xbc_w)

        attn = _moba(q, k, v, onehot)

        y3 = _ssd(
            xbc2.reshape(batch, seq, xbc_w), dt2.reshape(batch, seq, LANES),
            z2.reshape(batch, seq, ssm_w),
            _pad_lanes(dt_bias[i]), _pad_lanes(a_log[i]),
            jnp.repeat(d_skip[i], SSM_HEAD_DIM)[None, :], ssm_norm[i][None, :], tri, ltri,
            ssm_w=ssm_w)

        x2 = _proj_out(attn, y3.reshape(batch * seq, ssm_w), attn_norm[i][None, :],
                       w_out[i].astype(BF16), x2, seq=seq)

        x2 = _ffn(x2, ln2[i][None, :], w_up[i].astype(BF16), ffn_conv_w[i], ffn_conv_b[i][None, :],
                  w_down[i].astype(BF16), final_norm[None, :], seq=seq, final=(i == depth - 1))
    return x2.reshape(batch, seq, d_model)
```

```python
import functools

import numpy as np
import jax
import jax.numpy as jnp
from jax import lax
from jax.experimental import pallas as pl
from jax.experimental.pallas import tpu as pltpu

F32 = jnp.float32
BF16 = jnp.bfloat16

NORM_EPS = 1e-6
NEG_INF = -1e30
LOG2E = 1.4426950408889634
ROPE_THETA = 10000.0

ATTN_HEAD_DIM = 128
MOBA_BLOCK = 256
MOBA_TOPK = 3
MOBA_MAX_BLOCKS = 8

SSM_HEAD_DIM = 64
SSM_GROUPS = 8
SSM_STATE = 128
SSM_CONV = 4
SSD_CHUNK = 256
FFN_CONV = 3
ROW_CHUNK = 512
FFN_ROW_CHUNK = 256

LANES = 128
BF16_ROWS = 16
VMEM_LIMIT_BYTES = 56 * 1024 * 1024


def _cparams(*sem):
    return pltpu.CompilerParams(dimension_semantics=sem, vmem_limit_bytes=VMEM_LIMIT_BYTES)


def _row_tile(seq, pref):
    t = min(seq, pref)
    assert seq % t == 0
    return t


def _rms(xf, g):
    ms = jnp.mean(xf * xf, axis=-1, keepdims=True)
    return xf * lax.rsqrt(ms + NORM_EPS) * g


def _silu(x):
    return x * (1.0 / (1.0 + jnp.exp(-x)))


def _nt_dot(a, b):
    return lax.dot_general(a, b, (((1,), (1,)), ((), ())), preferred_element_type=F32)


def _proj_in_kernel(x_ref, ln_ref, w_ref, wdt_ref, cos_ref, sin_ref, cw_ref, cb_ref,
                    q_ref, k_ref, v_ref, z_ref, xbc_ref, dt_ref,
                    h_ref, halo_ref, *, nq, nz, hpt, nsb):
    i = pl.program_id(0)
    j = pl.program_id(1)
    tm = x_ref.shape[0]

    @pl.when(j == 0)
    def _():
        h = _rms(x_ref[...], ln_ref[...]).astype(BF16)
        h_ref[...] = h
        dt_ref[...] = jnp.dot(h, wdt_ref[...], preferred_element_type=F32)

    rc = min(tm, ROW_CHUNK)
    starts = range(0, tm, rc)

    def mm_chunks():
        return [jnp.dot(h_ref[r0:r0 + rc, :], w_ref[...], preferred_element_type=F32) for r0 in starts]

    def rope_store(o_ref):
        accs = mm_chunks()
        for r0, acc in zip(starts, accs):
            cos = cos_ref[r0:r0 + rc, :]
            sin = sin_ref[r0:r0 + rc, :]
            for hh in range(hpt):
                a = acc[:, hh * LANES:(hh + 1) * LANES]
                o_ref[0, hh, r0:r0 + rc, :] = (
                    a * cos + pltpu.roll(a, ATTN_HEAD_DIM // 2, axis=1) * sin).astype(BF16)

    @pl.when(j < nq)
    def _():
        rope_store(q_ref)

    @pl.when((j >= nq) & (j < 2 * nq))
    def _():
        rope_store(k_ref)

    @pl.when((j >= 2 * nq) & (j < 3 * nq))
    def _():
        accs = mm_chunks()
        for r0, acc in zip(starts, accs):
            for hh in range(hpt):
                v_ref[0, hh, r0:r0 + rc, :] = acc[:, hh * LANES:(hh + 1) * LANES].astype(BF16)

    @pl.when((j >= 3 * nq) & (j < 3 * nq + nz))
    def _():
        accs = mm_chunks()
        for r0, acc in zip(starts, accs):
            z_ref[r0:r0 + rc, :] = _silu(acc).astype(BF16)

    @pl.when((i == 0) & (j == 0))
    def _():
        halo_ref[...] = jnp.zeros_like(halo_ref)

    @pl.when(j >= 3 * nq + nz)
    def _():
        jx = j - (3 * nq + nz)
        prev = jnp.where(i % nsb == 0, 0.0, halo_ref[jx])
        cw = cw_ref[...]
        row = lax.broadcasted_iota(jnp.int32, prev.shape, 0)
        accs = mm_chunks()
        for r0, acc in zip(starts, accs):
            conv = cb_ref[...] + cw[SSM_CONV - 1:SSM_CONV, :] * acc
            for sh in range(1, SSM_CONV):
                r = pltpu.roll(acc, sh, axis=0)
                head = jnp.where(row < sh, pltpu.roll(prev, sh, axis=0), r[0:8])
                xk = jnp.concatenate([head, r[8:]], axis=0)
                conv = conv + cw[SSM_CONV - 1 - sh:SSM_CONV - sh, :] * xk
            xbc_ref[r0:r0 + rc, :] = _silu(conv).astype(BF16)
            prev = acc[rc - 8:, :]
        halo_ref[jx] = prev


def _proj_in(x2, ln, w_main, w_dt, cos_t, sin_t, conv_w, conv_b, *, batch, seq, attn_w, ssm_w, xbc_w):
    T, D = x2.shape
    tm = _row_tile(seq, 1024)
    tn = 512
    nsb = seq // tm
    n_heads = attn_w // ATTN_HEAD_DIM
    hpt = tn // ATTN_HEAD_DIM
    nq = attn_w // tn
    nz = ssm_w // tn
    nx = xbc_w // tn
    nj = 3 * nq + nz + nx
    assert w_main.shape[0] == D and w_main.shape[1] >= nj * tn

    def clampj(lo, n):
        return lambda j: jnp.clip(j - lo, 0, n - 1)

    qj, kj, vj = clampj(0, nq), clampj(nq, nq), clampj(2 * nq, nq)
    zj, xj = clampj(3 * nq, nz), clampj(3 * nq + nz, nx)

    head_shape = jax.ShapeDtypeStruct((batch, n_heads, seq, ATTN_HEAD_DIM), BF16)

    def head_spec(fj):
        return pl.BlockSpec((1, hpt, tm, ATTN_HEAD_DIM), lambda i, j: (i // nsb, fj(j), i % nsb, 0))

    return pl.pallas_call(
        functools.partial(_proj_in_kernel, nq=nq, nz=nz, hpt=hpt, nsb=nsb),
        grid=(T // tm, nj),
        in_specs=[
            pl.BlockSpec((tm, D), lambda i, j: (i, 0)),
            pl.BlockSpec((1, D), lambda i, j: (0, 0)),
            pl.BlockSpec((D, tn), lambda i, j: (0, j)),
            pl.BlockSpec((D, LANES), lambda i, j: (0, 0)),
            pl.BlockSpec((tm, ATTN_HEAD_DIM), lambda i, j: (i % nsb, 0)),
            pl.BlockSpec((tm, ATTN_HEAD_DIM), lambda i, j: (i % nsb, 0)),
            pl.BlockSpec((SSM_CONV, tn), lambda i, j: (0, xj(j))),
            pl.BlockSpec((1, tn), lambda i, j: (0, xj(j))),
        ],
        out_specs=[
            head_spec(qj), head_spec(kj), head_spec(vj),
            pl.BlockSpec((tm, tn), lambda i, j: (i, zj(j))),
            pl.BlockSpec((tm, tn), lambda i, j: (i, xj(j))),
            pl.BlockSpec((tm, LANES), lambda i, j: (i, 0)),
        ],
        out_shape=[
            head_shape, head_shape, head_shape,
            jax.ShapeDtypeStruct((T, ssm_w), BF16),
            jax.ShapeDtypeStruct((T, xbc_w), BF16),
            jax.ShapeDtypeStruct((T, LANES), F32),
        ],
        scratch_shapes=[
            pltpu.VMEM((tm, D), BF16),
            pltpu.VMEM((nx, 8, tn), F32),
        ],
        compiler_params=_cparams("arbitrary", "arbitrary"),
        name="proj_in",
    )(x2, ln, w_main, w_dt, cos_t, sin_t, conv_w, conv_b)


def _moba_kernel(q_ref, k_ref, v_ref, oh_ref, o_ref, qa_ref, *, nb, scale):
    seq = nb * MOBA_BLOCK
    nbp = MOBA_MAX_BLOCKS
    blk = MOBA_BLOCK

    kf = k_ref[0, 0].astype(F32)
    rows = [jnp.sum(kf[b * blk:(b + 1) * blk], axis=0, keepdims=True) for b in range(nb)]
    if nb < nbp:
        rows.append(jnp.zeros((nbp - nb, ATTN_HEAD_DIM), F32))
    kmean = jnp.concatenate(rows, axis=0) * (1.0 / blk)
    k_hi = kmean.astype(BF16)
    k_lo = (kmean - k_hi.astype(F32)).astype(BF16)
    q = q_ref[0, 0]
    g2 = _nt_dot(jnp.concatenate([k_hi, k_lo], axis=0), q)
    gate = g2[0:nbp] + g2[nbp:2 * nbp]
    own = jnp.right_shift(lax.broadcasted_iota(jnp.int32, (nbp, seq), 1), blk.bit_length() - 1)
    kb = lax.broadcasted_iota(jnp.int32, (nbp, seq), 0)
    rank = jnp.zeros((nbp, seq), F32)
    for b in range(nb):
        gb = gate[b:b + 1, :]
        beats = (b < own) & ((gb > gate) | ((gb == gate) & (b < kb)))
        rank = rank + jnp.where(beats, 1.0, 0.0)
    allowed = ((kb < own) & (rank < MOBA_TOPK)) | (kb == own)
    bias_t = jnp.where(allowed, 0.0, NEG_INF)
    bias_t = jnp.concatenate([bias_t, jnp.zeros((LANES - nbp, seq), F32)], axis=0)
    qa_ref[:, 0:ATTN_HEAD_DIM] = q
    qa_ref[:, ATTN_HEAD_DIM:] = bias_t.T.astype(BF16)

    qi = lax.broadcasted_iota(jnp.int32, (blk, blk), 0)
    ki = lax.broadcasted_iota(jnp.int32, (blk, blk), 1)
    causal = ki <= qi
    c2 = scale * LOG2E
    def scores(i):
        hi = (i + 1) * blk
        ka = jnp.concatenate([k_ref[0, 0, 0:hi, :], oh_ref[0:hi, :]], axis=1)
        return _nt_dot(qa_ref[i * blk:hi, :], ka)

    s_next = scores(0)
    for i in range(nb):
        hi = (i + 1) * blk
        s = s_next
        if i + 1 < nb:
            s_next = scores(i + 1)
        s_own = jnp.where(causal, s[:, i * blk:], NEG_INF)
        s = jnp.concatenate([s[:, 0:i * blk], s_own], axis=1) if i else s_own
        m = jnp.max(s, axis=1, keepdims=True)
        p = jnp.exp2((s - m) * c2)
        l = jnp.sum(p, axis=1, keepdims=True)
        acc = jnp.dot(p.astype(BF16), v_ref[0, 0, 0:hi, :], preferred_element_type=F32)
        o_ref[0, 0, i * blk:hi, :] = (acc / l).astype(BF16)


def _moba(q, k, v, onehot):
    batch, n_heads, seq, dh = q.shape
    nb = seq // MOBA_BLOCK
    assert seq % MOBA_BLOCK == 0 and nb <= MOBA_MAX_BLOCKS and dh == ATTN_HEAD_DIM
    full = pl.BlockSpec((1, 1, seq, dh), lambda b, h: (b, h, 0, 0))
    return pl.pallas_call(
        functools.partial(_moba_kernel, nb=nb, scale=dh ** -0.5),
        grid=(batch, n_heads),
        in_specs=[full, full, full, pl.BlockSpec((seq, LANES), lambda b, h: (0, 0))],
        out_specs=full,
        out_shape=jax.ShapeDtypeStruct((batch, n_heads, seq, dh), BF16),
        scratch_shapes=[pltpu.VMEM((seq, 2 * dh), BF16)],
        compiler_params=_cparams("parallel", "parallel"),
        name="moba",
    )(q, k, v, onehot)


def _ssd_kernel(xx_ref, xb_ref, xc_ref, dt_ref, dtb_ref, alog_ref, dskip_ref, z_ref, nw_ref,
                tri_ref, ltri_ref, y_ref,
                st_ref, acsc_ref, rows_ref, rowt_ref, *, hpg):
    c = pl.program_id(1)
    g = pl.program_id(2)
    L = SSD_CHUNK
    P = SSM_HEAD_DIM
    N = SSM_STATE
    gw = hpg * P
    n_groups = st_ref.shape[0]

    @pl.when(g == 0)
    def _():
        dtv = dt_ref[0] + dtb_ref[...]
        dtv = jnp.maximum(dtv, 0.0) + jnp.log1p(jnp.exp(-jnp.abs(dtv)))
        a = dtv * (-jnp.exp(alog_ref[...]))
        a_hi = a.astype(BF16)
        r1 = a - a_hi.astype(F32)
        a_mid = r1.astype(BF16)
        a_lo = (r1 - a_mid.astype(F32)).astype(BF16)
        cs3 = jnp.dot(ltri_ref[...], jnp.concatenate([a_hi, a_mid, a_lo], axis=1),
                      preferred_element_type=F32)
        acs2 = (cs3[:, 0:LANES] + cs3[:, LANES:2 * LANES] + cs3[:, 2 * LANES:]) * LOG2E
        rowt_ref[...] = (acs2 - jnp.log2(dtv)).T
        for gg in range(n_groups):
            sh = (LANES - hpg * gg) % LANES
            acsc_ref[gg] = pltpu.roll(acs2, sh, axis=1) if sh else acs2
            rows_ref[gg, 0:hpg, :] = rowt_ref[hpg * gg:hpg * (gg + 1), :]

    @pl.when(c == 0)
    def _():
        st_ref[g] = jnp.zeros((N, gw), F32)

    xs_b = xx_ref[0]
    bm_b = xb_ref[0]
    cm_b = xc_ref[0]
    cb_mat = _nt_dot(cm_b, bm_b)
    bm_t = bm_b.astype(F32).T
    acsc = acsc_ref[g]
    tri = tri_ref[...]
    lane_head = jnp.right_shift(lax.broadcasted_iota(jnp.int32, (1, gw), 1), P.bit_length() - 1)
    e_col = jnp.exp2(acsc)
    a_end = acsc[L - 1:L, :]

    mp, bw, xm = [], [], []
    dfs = jnp.zeros((L, gw), F32)
    dch = jnp.zeros((1, gw), F32)
    for r in range(hpg):
        a_col = acsc[:, r:r + 1]
        a_row = rows_ref[g, r:r + 1, :]
        a_last = a_end[:, r:r + 1]
        mp.append((cb_mat * jnp.exp2(a_col - a_row + tri)).astype(BF16))
        bw.append((bm_t * jnp.exp2(a_last - a_row)).astype(BF16))
        hmask = lane_head == r
        xm.append(jnp.where(hmask, xs_b, jnp.zeros_like(xs_b)))
        dfs = jnp.where(hmask, jnp.broadcast_to(e_col[:, r:r + 1], (L, gw)), dfs)
        dch = jnp.where(hmask, jnp.broadcast_to(jnp.exp2(a_last), (1, gw)), dch)
    mp = jnp.concatenate(mp, axis=1)
    bw = jnp.concatenate(bw, axis=1)
    xm = jnp.concatenate(xm, axis=0)

    st_old = st_ref[g]
    y = jnp.dot(mp, xm, preferred_element_type=F32)
    y = y + jnp.dot(cm_b, st_old.astype(BF16), preferred_element_type=F32) * dfs
    y = y + dskip_ref[...] * xs_b.astype(F32)
    st_ref[g] = st_old * dch + jnp.dot(bw, xm, preferred_element_type=F32)

    hg = y * z_ref[0].astype(F32)
    y_ref[0] = (_rms(hg, nw_ref[...])).astype(BF16)


def _ssd(xbc3, dt3, z3, dtb, alog, dskip, nw, tri, ltri, *, ssm_w):
    batch, seq, xbc_w = xbc3.shape
    L = SSD_CHUNK
    N = SSM_STATE
    G = SSM_GROUPS
    assert seq % L == 0
    nc = seq // L
    gw = ssm_w // G
    hpg = gw // SSM_HEAD_DIM
    assert hpg <= 8 and xbc_w == ssm_w + 2 * G * N and gw % LANES == 0
    b_off = ssm_w // N
    c_off = b_off + G

    return pl.pallas_call(
        functools.partial(_ssd_kernel, hpg=hpg),
        grid=(batch, nc, G),
        in_specs=[
            pl.BlockSpec((1, L, gw), lambda b, c, g: (b, c, g)),
            pl.BlockSpec((1, L, N), lambda b, c, g: (b, c, b_off + g)),
            pl.BlockSpec((1, L, N), lambda b, c, g: (b, c, c_off + g)),
            pl.BlockSpec((1, L, LANES), lambda b, c, g: (b, c, 0)),
            pl.BlockSpec((1, LANES), lambda b, c, g: (0, 0)),
            pl.BlockSpec((1, LANES), lambda b, c, g: (0, 0)),
            pl.BlockSpec((1, gw), lambda b, c, g: (0, g)),
            pl.BlockSpec((1, L, gw), lambda b, c, g: (b, c, g)),
            pl.BlockSpec((1, gw), lambda b, c, g: (0, g)),
            pl.BlockSpec((L, L), lambda b, c, g: (0, 0)),
            pl.BlockSpec((L, L), lambda b, c, g: (0, 0)),
        ],
        out_specs=pl.BlockSpec((1, L, gw), lambda b, c, g: (b, c, g)),
        out_shape=jax.ShapeDtypeStruct((batch, seq, ssm_w), BF16),
        scratch_shapes=[
            pltpu.VMEM((G, N, gw), F32),
            pltpu.VMEM((G, L, LANES), F32),
            pltpu.VMEM((G, 8, L), F32),
            pltpu.VMEM((LANES, L), F32),
        ],
        compiler_params=_cparams("parallel", "arbitrary", "arbitrary"),
        name="ssd",
    )(xbc3, xbc3, xbc3, dt3, dtb, alog, dskip, z3, nw, tri, ltri)


def _proj_out_kernel(attn_ref, y_ref, an_ref, w_ref, x_ref, o_ref, cat_ref, *, n_heads):
    j = pl.program_id(1)
    aw = n_heads * ATTN_HEAD_DIM

    @pl.when(j == 0)
    def _():
        a = jnp.concatenate([attn_ref[0, hh].astype(F32) for hh in range(n_heads)], axis=1)
        cat_ref[:, 0:aw] = _rms(a, an_ref[...]).astype(BF16)
        cat_ref[:, aw:] = y_ref[...]

    tm = x_ref.shape[0]
    rc = min(tm, ROW_CHUNK)
    starts = range(0, tm, rc)
    accs = [jnp.dot(cat_ref[r0:r0 + rc, :], w_ref[...], preferred_element_type=F32) for r0 in starts]
    for r0, acc in zip(starts, accs):
        o_ref[r0:r0 + rc, :] = x_ref[r0:r0 + rc, :] + acc


def _proj_out(attn, y2, an, w_out, x2, *, seq):
    batch, n_heads, _, dh = attn.shape
    T, D = x2.shape
    aw = n_heads * dh
    sw = y2.shape[1]
    tm = _row_tile(seq, 1024)
    tn = 512
    nsb = seq // tm
    return pl.pallas_call(
        functools.partial(_proj_out_kernel, n_heads=n_heads),
        grid=(T // tm, D // tn),
        in_specs=[
            pl.BlockSpec((1, n_heads, tm, dh), lambda i, j: (i // nsb, 0, i % nsb, 0)),
            pl.BlockSpec((tm, sw), lambda i, j: (i, 0)),
            pl.BlockSpec((1, aw), lambda i, j: (0, 0)),
            pl.BlockSpec((aw + sw, tn), lambda i, j: (0, j)),
            pl.BlockSpec((tm, tn), lambda i, j: (i, j)),
        ],
        out_specs=pl.BlockSpec((tm, tn), lambda i, j: (i, j)),
        out_shape=jax.ShapeDtypeStruct((T, D), F32),
        scratch_shapes=[pltpu.VMEM((tm, aw + sw), BF16)],
        compiler_params=_cparams("parallel", "arbitrary"),
        name="proj_out",
    )(attn, y2, an, w_out, x2)


def _ffn_kernel(x_ref, ln_ref, wg_ref, wv_ref, cwg_ref, cwv_ref, cbg_ref, cbv_ref, wd_ref, fn_ref, o_ref,
                h_ref, halo_ref, ug_ref, uv_ref, *, nsb, final):
    i = pl.program_id(0)
    j = pl.program_id(1)
    tm = x_ref.shape[0]

    @pl.when((i == 0) & (j == 0))
    def _():
        halo_ref[...] = jnp.zeros_like(halo_ref)

    @pl.when(j == 0)
    def _():
        x = x_ref[...]
        h_ref[...] = _rms(x, ln_ref[...]).astype(BF16)
        o_ref[...] = x

    first = i % nsb == 0
    rc = min(tm, FFN_ROW_CHUNK)
    starts = range(0, tm, rc)
    ups = [(jnp.dot(h_ref[r0:r0 + rc, :], wg_ref[...], preferred_element_type=F32),
            jnp.dot(h_ref[r0:r0 + rc, :], wv_ref[...], preferred_element_type=F32)) for r0 in starts]
    ug_ref[0:8, :] = jnp.where(first, 0.0, halo_ref[j, 0])
    uv_ref[0:8, :] = jnp.where(first, 0.0, halo_ref[j, 1])
    halo_ref[j, 0] = ups[-1][0][rc - 8:, :]
    halo_ref[j, 1] = ups[-1][1][rc - 8:, :]

    def conv(u_ref, cw_ref, cb_ref, r0):
        out = cb_ref[...]
        for t in range(FFN_CONV):
            o = 8 - (FFN_CONV - 1) + t + r0
            out = out + cw_ref[t:t + 1, :] * u_ref[o:o + rc, :]
        return out

    for r0, (g, v) in zip(starts, ups):
        ug_ref[8 + r0:8 + r0 + rc, :] = g
        uv_ref[8 + r0:8 + r0 + rc, :] = v
        act = (_silu(conv(ug_ref, cwg_ref, cbg_ref, r0)) * conv(uv_ref, cwv_ref, cbv_ref, r0)).astype(BF16)
        o_ref[r0:r0 + rc, :] += jnp.dot(act, wd_ref[...], preferred_element_type=F32)

    if final:
        @pl.when(j == pl.num_programs(1) - 1)
        def _():
            o_ref[...] = _rms(o_ref[...], fn_ref[...])


def _ffn(x2, ln, w_up, conv_w, conv_b, w_down, fn, *, seq, final):
    T, D = x2.shape
    dff = w_down.shape[0]
    tm = _row_tile(seq, 512)
    tf = 512
    assert dff % tf == 0 and w_up.shape == (D, 2 * dff)
    nsb = seq // tm
    nf = dff // tf
    return pl.pallas_call(
        functools.partial(_ffn_kernel, nsb=nsb, final=final),
        grid=(T // tm, nf),
        in_specs=[
            pl.BlockSpec((tm, D), lambda i, j: (i, 0)),
            pl.BlockSpec((1, D), lambda i, j: (0, 0)),
            pl.BlockSpec((D, tf), lambda i, j: (0, j)),
            pl.BlockSpec((D, tf), lambda i, j: (0, nf + j)),
            pl.BlockSpec((FFN_CONV, tf), lambda i, j: (0, j)),
            pl.BlockSpec((FFN_CONV, tf), lambda i, j: (0, nf + j)),
            pl.BlockSpec((1, tf), lambda i, j: (0, j)),
            pl.BlockSpec((1, tf), lambda i, j: (0, nf + j)),
            pl.BlockSpec((tf, D), lambda i, j: (j, 0)),
            pl.BlockSpec((1, D), lambda i, j: (0, 0)),
        ],
        out_specs=pl.BlockSpec((tm, D), lambda i, j: (i, 0)),
        out_shape=jax.ShapeDtypeStruct((T, D), F32),
        scratch_shapes=[
            pltpu.VMEM((tm, D), BF16),
            pltpu.VMEM((nf, 2, 8, tf), F32),
            pltpu.VMEM((tm + 8, tf), F32),
            pltpu.VMEM((tm + 8, tf), F32),
        ],
        compiler_params=_cparams("arbitrary", "arbitrary"),
        name="ffn_final" if final else "ffn",
    )(x2, ln, w_up, w_up, conv_w, conv_w, conv_b, conv_b, w_down, fn)


def _rope_tables(seq):
    half = ATTN_HEAD_DIM // 2
    inv_freq = jnp.power(ROPE_THETA, -jnp.arange(half, dtype=F32) / half)
    ang = jnp.arange(seq, dtype=F32)[:, None] * inv_freq[None, :]
    cos, sin = jnp.cos(ang), jnp.sin(ang)
    return jnp.concatenate([cos, cos], axis=-1), jnp.concatenate([-sin, sin], axis=-1)


def _chunk_constants():
    L = SSD_CHUNK
    low = np.tril(np.ones((L, L), np.float32))
    tri = jnp.asarray(np.where(low > 0, 0.0, NEG_INF).astype(np.float32))
    ltri = jnp.asarray(low).astype(BF16)
    return tri, ltri


def _block_onehot(seq):
    oh = np.zeros((seq, LANES), np.float32)
    oh[np.arange(seq), np.arange(seq) // MOBA_BLOCK] = 1.0
    return jnp.asarray(oh).astype(BF16)


def _pad_lanes(v):
    return jnp.pad(v, (0, LANES - v.shape[0]))[None, :]


def kernel(x, ln1, w_in, attn_norm, ssm_conv_w, ssm_conv_b, dt_bias, a_log, d_skip, ssm_norm, w_out, ln2, w_up, ffn_conv_w, ffn_conv_b, w_down, final_norm):
    batch, seq, d_model = x.shape
    depth = ln1.shape[0]
    attn_w = attn_norm.shape[1]
    ssm_w = ssm_norm.shape[1]
    xbc_w = ssm_conv_w.shape[2]
    n_ssm_heads = a_log.shape[1]
    assert ssm_w // n_ssm_heads == SSM_HEAD_DIM and n_ssm_heads <= LANES
    main_w = 3 * attn_w + ssm_w + xbc_w
    cos_t, sin_t = _rope_tables(seq)
    tri, ltri = _chunk_constants()
    onehot = _block_onehot(seq)

    x2 = x.reshape(batch * seq, d_model)
    for i in range(depth):
        w_main = w_in[i].astype(BF16)
        w_dt = jnp.pad(w_main[:, main_w:], ((0, 0), (0, LANES - n_ssm_heads)))
        q, k, v, z2, xbc2, dt2 = _proj_in(
            x2, ln1[i][None, :], w_main, w_dt, cos_t, sin_t, ssm_conv_w[i], ssm_conv_b[i][None, :],
            batch=batch, seq=seq, attn_w=attn_w, ssm_w=ssm_w, xbc_w=xbc_w)

        attn = _moba(q, k, v, onehot)

        y3 = _ssd(
            xbc2.reshape(batch, seq, xbc_w), dt2.reshape(batch, seq, LANES),
            z2.reshape(batch, seq, ssm_w),
            _pad_lanes(dt_bias[i]), _pad_lanes(a_log[i]),
            jnp.repeat(d_skip[i], SSM_HEAD_DIM)[None, :], ssm_norm[i][None, :], tri, ltri,
            ssm_w=ssm_w)

        x2 = _proj_out(attn, y3.reshape(batch * seq, ssm_w), attn_norm[i][None, :],
                       w_out[i].astype(BF16), x2, seq=seq)

        x2 = _ffn(x2, ln2[i][None, :], w_up[i].astype(BF16), ffn_conv_w[i], ffn_conv_b[i][None, :],
                  w_down[i].astype(BF16), final_norm[None, :], seq=seq, final=(i == depth - 1))
    return x2.reshape(batch, seq, d_model)
```

```python
import functools

import numpy as np
import jax
import jax.numpy as jnp
from jax import lax
from jax.experimental import pallas as pl
from jax.experimental.pallas import tpu as pltpu

F32 = jnp.float32
BF16 = jnp.bfloat16

NORM_EPS = 1e-6
NEG_INF = -1e30
LOG2E = 1.4426950408889634
ROPE_THETA = 10000.0

ATTN_HEAD_DIM = 128
MOBA_BLOCK = 256
MOBA_TOPK = 3
MOBA_MAX_BLOCKS = 8

SSM_HEAD_DIM = 64
SSM_GROUPS = 8
SSM_STATE = 128
SSM_CONV = 4
SSD_CHUNK = 256
SSD_GROUPS_PER_STEP = 2
FFN_CONV = 3
ROW_CHUNK = 512
FFN_ROW_CHUNK = 256

LANES = 128
BF16_ROWS = 16
VMEM_LIMIT_BYTES = 56 * 1024 * 1024


def _cparams(*sem):
    return pltpu.CompilerParams(dimension_semantics=sem, vmem_limit_bytes=VMEM_LIMIT_BYTES)


def _row_tile(seq, pref):
    t = min(seq, pref)
    assert seq % t == 0
    return t


def _rms(xf, g):
    ms = jnp.mean(xf * xf, axis=-1, keepdims=True)
    return xf * lax.rsqrt(ms + NORM_EPS) * g


def _silu(x):
    return x * (1.0 / (1.0 + jnp.exp(-x)))


def _nt_dot(a, b):
    return lax.dot_general(a, b, (((1,), (1,)), ((), ())), preferred_element_type=F32)


def _proj_in_kernel(x_ref, ln_ref, w_ref, wdt_ref, cos_ref, sin_ref, cw_ref, cb_ref,
                    q_ref, k_ref, v_ref, z_ref, xbc_ref, dt_ref,
                    h_ref, halo_ref, *, nq, nz, hpt, nsb):
    i = pl.program_id(0)
    j = pl.program_id(1)
    tm = x_ref.shape[0]

    @pl.when(j == 0)
    def _():
        h = _rms(x_ref[...], ln_ref[...]).astype(BF16)
        h_ref[...] = h
        dt_ref[...] = jnp.dot(h, wdt_ref[...], preferred_element_type=F32)

    rc = min(tm, ROW_CHUNK)
    starts = range(0, tm, rc)

    def mm_chunks():
        return [jnp.dot(h_ref[r0:r0 + rc, :], w_ref[...], preferred_element_type=F32) for r0 in starts]

    def rope_store(o_ref):
        accs = mm_chunks()
        for r0, acc in zip(starts, accs):
            cos = cos_ref[r0:r0 + rc, :]
            sin = sin_ref[r0:r0 + rc, :]
            for hh in range(hpt):
                a = acc[:, hh * LANES:(hh + 1) * LANES]
                o_ref[0, hh, r0:r0 + rc, :] = (
                    a * cos + pltpu.roll(a, ATTN_HEAD_DIM // 2, axis=1) * sin).astype(BF16)

    @pl.when(j < nq)
    def _():
        rope_store(q_ref)

    @pl.when((j >= nq) & (j < 2 * nq))
    def _():
        rope_store(k_ref)

    @pl.when((j >= 2 * nq) & (j < 3 * nq))
    def _():
        accs = mm_chunks()
        for r0, acc in zip(starts, accs):
            for hh in range(hpt):
                v_ref[0, hh, r0:r0 + rc, :] = acc[:, hh * LANES:(hh + 1) * LANES].astype(BF16)

    @pl.when((j >= 3 * nq) & (j < 3 * nq + nz))
    def _():
        accs = mm_chunks()
        for r0, acc in zip(starts, accs):
            z_ref[r0:r0 + rc, :] = _silu(acc).astype(BF16)

    @pl.when((i == 0) & (j == 0))
    def _():
        halo_ref[...] = jnp.zeros_like(halo_ref)

    @pl.when(j >= 3 * nq + nz)
    def _():
        jx = j - (3 * nq + nz)
        prev = jnp.where(i % nsb == 0, 0.0, halo_ref[jx])
        cw = cw_ref[...]
        row = lax.broadcasted_iota(jnp.int32, prev.shape, 0)
        accs = mm_chunks()
        for r0, acc in zip(starts, accs):
            conv = cb_ref[...] + cw[SSM_CONV - 1:SSM_CONV, :] * acc
            for sh in range(1, SSM_CONV):
                r = pltpu.roll(acc, sh, axis=0)
                head = jnp.where(row < sh, pltpu.roll(prev, sh, axis=0), r[0:8])
                xk = jnp.concatenate([head, r[8:]], axis=0)
                conv = conv + cw[SSM_CONV - 1 - sh:SSM_CONV - sh, :] * xk
            xbc_ref[r0:r0 + rc, :] = _silu(conv).astype(BF16)
            prev = acc[rc - 8:, :]
        halo_ref[jx] = prev


def _proj_in(x2, ln, w_main, w_dt, cos_t, sin_t, conv_w, conv_b, *, batch, seq, attn_w, ssm_w, xbc_w):
    T, D = x2.shape
    tm = _row_tile(seq, 1024)
    tn = 512
    nsb = seq // tm
    n_heads = attn_w // ATTN_HEAD_DIM
    hpt = tn // ATTN_HEAD_DIM
    nq = attn_w // tn
    nz = ssm_w // tn
    nx = xbc_w // tn
    nj = 3 * nq + nz + nx
    assert w_main.shape[0] == D and w_main.shape[1] >= nj * tn

    def clampj(lo, n):
        return lambda j: jnp.clip(j - lo, 0, n - 1)

    qj, kj, vj = clampj(0, nq), clampj(nq, nq), clampj(2 * nq, nq)
    zj, xj = clampj(3 * nq, nz), clampj(3 * nq + nz, nx)

    head_shape = jax.ShapeDtypeStruct((batch, n_heads, seq, ATTN_HEAD_DIM), BF16)

    def head_spec(fj):
        return pl.BlockSpec((1, hpt, tm, ATTN_HEAD_DIM), lambda i, j: (i // nsb, fj(j), i % nsb, 0))

    return pl.pallas_call(
        functools.partial(_proj_in_kernel, nq=nq, nz=nz, hpt=hpt, nsb=nsb),
        grid=(T // tm, nj),
        in_specs=[
            pl.BlockSpec((tm, D), lambda i, j: (i, 0)),
            pl.BlockSpec((1, D), lambda i, j: (0, 0)),
            pl.BlockSpec((D, tn), lambda i, j: (0, j)),
            pl.BlockSpec((D, LANES), lambda i, j: (0, 0)),
            pl.BlockSpec((tm, ATTN_HEAD_DIM), lambda i, j: (i % nsb, 0)),
            pl.BlockSpec((tm, ATTN_HEAD_DIM), lambda i, j: (i % nsb, 0)),
            pl.BlockSpec((SSM_CONV, tn), lambda i, j: (0, xj(j))),
            pl.BlockSpec((1, tn), lambda i, j: (0, xj(j))),
        ],
        out_specs=[
            head_spec(qj), head_spec(kj), head_spec(vj),
            pl.BlockSpec((tm, tn), lambda i, j: (i, zj(j))),
            pl.BlockSpec((tm, tn), lambda i, j: (i, xj(j))),
            pl.BlockSpec((tm, LANES), lambda i, j: (i, 0)),
        ],
        out_shape=[
            head_shape, head_shape, head_shape,
            jax.ShapeDtypeStruct((T, ssm_w), BF16),
            jax.ShapeDtypeStruct((T, xbc_w), BF16),
            jax.ShapeDtypeStruct((T, LANES), F32),
        ],
        scratch_shapes=[
            pltpu.VMEM((tm, D), BF16),
            pltpu.VMEM((nx, 8, tn), F32),
        ],
        compiler_params=_cparams("arbitrary", "arbitrary"),
        name="proj_in",
    )(x2, ln, w_main, w_dt, cos_t, sin_t, conv_w, conv_b)


def _moba_kernel(q_ref, k_ref, v_ref, oh_ref, o_ref, qa_ref, *, nb, scale):
    seq = nb * MOBA_BLOCK
    nbp = MOBA_MAX_BLOCKS
    blk = MOBA_BLOCK

    kf = k_ref[0, 0].astype(F32)
    rows = [jnp.sum(kf[b * blk:(b + 1) * blk], axis=0, keepdims=True) for b in range(nb)]
    if nb < nbp:
        rows.append(jnp.zeros((nbp - nb, ATTN_HEAD_DIM), F32))
    kmean = jnp.concatenate(rows, axis=0) * (1.0 / blk)
    k_hi = kmean.astype(BF16)
    k_lo = (kmean - k_hi.astype(F32)).astype(BF16)
    q = q_ref[0, 0]
    g2 = _nt_dot(jnp.concatenate([k_hi, k_lo], axis=0), q)
    gate = g2[0:nbp] + g2[nbp:2 * nbp]
    own = jnp.right_shift(lax.broadcasted_iota(jnp.int32, (nbp, seq), 1), blk.bit_length() - 1)
    kb = lax.broadcasted_iota(jnp.int32, (nbp, seq), 0)
    rank = jnp.zeros((nbp, seq), F32)
    for b in range(nb):
        gb = gate[b:b + 1, :]
        beats = (b < own) & ((gb > gate) | ((gb == gate) & (b < kb)))
        rank = rank + jnp.where(beats, 1.0, 0.0)
    allowed = ((kb < own) & (rank < MOBA_TOPK)) | (kb == own)
    bias_t = jnp.where(allowed, 0.0, NEG_INF)
    bias_t = jnp.concatenate([bias_t, jnp.zeros((LANES - nbp, seq), F32)], axis=0)
    qa_ref[:, 0:ATTN_HEAD_DIM] = q
    qa_ref[:, ATTN_HEAD_DIM:] = bias_t.T.astype(BF16)

    qi = lax.broadcasted_iota(jnp.int32, (blk, blk), 0)
    ki = lax.broadcasted_iota(jnp.int32, (blk, blk), 1)
    causal = ki <= qi
    c2 = scale * LOG2E
    def scores(i):
        hi = (i + 1) * blk
        ka = jnp.concatenate([k_ref[0, 0, 0:hi, :], oh_ref[0:hi, :]], axis=1)
        return _nt_dot(qa_ref[i * blk:hi, :], ka)

    s_next = scores(0)
    for i in range(nb):
        hi = (i + 1) * blk
        s = s_next
        if i + 1 < nb:
            s_next = scores(i + 1)
        s_own = jnp.where(causal, s[:, i * blk:], NEG_INF)
        s = jnp.concatenate([s[:, 0:i * blk], s_own], axis=1) if i else s_own
        m = jnp.max(s, axis=1, keepdims=True)
        p = jnp.exp2((s - m) * c2)
        l = jnp.sum(p, axis=1, keepdims=True)
        acc = jnp.dot(p.astype(BF16), v_ref[0, 0, 0:hi, :], preferred_element_type=F32)
        o_ref[0, 0, i * blk:hi, :] = (acc / l).astype(BF16)


def _moba(q, k, v, onehot):
    batch, n_heads, seq, dh = q.shape
    nb = seq // MOBA_BLOCK
    assert seq % MOBA_BLOCK == 0 and nb <= MOBA_MAX_BLOCKS and dh == ATTN_HEAD_DIM
    full = pl.BlockSpec((1, 1, seq, dh), lambda b, h: (b, h, 0, 0))
    return pl.pallas_call(
        functools.partial(_moba_kernel, nb=nb, scale=dh ** -0.5),
        grid=(batch, n_heads),
        in_specs=[full, full, full, pl.BlockSpec((seq, LANES), lambda b, h: (0, 0))],
        out_specs=full,
        out_shape=jax.ShapeDtypeStruct((batch, n_heads, seq, dh), BF16),
        scratch_shapes=[pltpu.VMEM((seq, 2 * dh), BF16)],
        compiler_params=_cparams("parallel", "parallel"),
        name="moba",
    )(q, k, v, onehot)


def _ssd_kernel(xx_ref, xb_ref, xc_ref, dt_ref, dtb_ref, alog_ref, dskip_ref, z_ref, nw_ref,
                tri_ref, ltri_ref, y_ref,
                st_ref, acsc_ref, rows_ref, rowt_ref, *, hpg, gps):
    c = pl.program_id(1)
    gp = pl.program_id(2)
    L = SSD_CHUNK
    P = SSM_HEAD_DIM
    N = SSM_STATE
    gw = hpg * P
    n_groups = st_ref.shape[0]

    @pl.when(gp == 0)
    def _():
        dtv = dt_ref[0] + dtb_ref[...]
        dtv = jnp.maximum(dtv, 0.0) + jnp.log1p(jnp.exp(-jnp.abs(dtv)))
        a = dtv * (-jnp.exp(alog_ref[...]))
        a_hi = a.astype(BF16)
        r1 = a - a_hi.astype(F32)
        a_mid = r1.astype(BF16)
        a_lo = (r1 - a_mid.astype(F32)).astype(BF16)
        cs3 = jnp.dot(ltri_ref[...], jnp.concatenate([a_hi, a_mid, a_lo], axis=1),
                      preferred_element_type=F32)
        acs2 = (cs3[:, 0:LANES] + cs3[:, LANES:2 * LANES] + cs3[:, 2 * LANES:]) * LOG2E
        rowt_ref[...] = (acs2 - jnp.log2(dtv)).T
        for gg in range(n_groups):
            sh = (LANES - hpg * gg) % LANES
            acsc_ref[gg] = pltpu.roll(acs2, sh, axis=1) if sh else acs2
            rows_ref[gg, 0:hpg, :] = rowt_ref[hpg * gg:hpg * (gg + 1), :]

    @pl.when(c == 0)
    def _():
        for u in range(gps):
            st_ref[gp * gps + u] = jnp.zeros((N, gw), F32)

    tri = tri_ref[...]
    lane_head = jnp.right_shift(lax.broadcasted_iota(jnp.int32, (1, gw), 1), P.bit_length() - 1)

    U = range(gps)
    gs = [gp * gps + u for u in U]
    xs_b = [xx_ref[0, :, u * gw:(u + 1) * gw] for u in U]
    bm_b = [xb_ref[0, :, u * N:(u + 1) * N] for u in U]
    cm_b = [xc_ref[0, :, u * N:(u + 1) * N] for u in U]
    cb_mat = [_nt_dot(cm_b[u], bm_b[u]) for u in U]
    bm_t = [bm_b[u].astype(F32).T for u in U]
    acsc = [acsc_ref[gs[u]] for u in U]
    e_col = [jnp.exp2(acsc[u]) for u in U]
    a_end = [acsc[u][L - 1:L, :] for u in U]

    mp = [[] for _ in U]
    bw = [[] for _ in U]
    xm = [[] for _ in U]
    dfs = [jnp.zeros((L, gw), F32) for _ in U]
    dch = [jnp.zeros((1, gw), F32) for _ in U]
    for r in range(hpg):
        hmask = lane_head == r
        for u in U:
            a_col = acsc[u][:, r:r + 1]
            a_row = rows_ref[gs[u], r:r + 1, :]
            a_last = a_end[u][:, r:r + 1]
            mp[u].append((cb_mat[u] * jnp.exp2(a_col - a_row + tri)).astype(BF16))
            bw[u].append((bm_t[u] * jnp.exp2(a_last - a_row)).astype(BF16))
            xm[u].append(jnp.where(hmask, xs_b[u], jnp.zeros_like(xs_b[u])))
            dfs[u] = jnp.where(hmask, jnp.broadcast_to(e_col[u][:, r:r + 1], (L, gw)), dfs[u])
            dch[u] = jnp.where(hmask, jnp.broadcast_to(jnp.exp2(a_last), (1, gw)), dch[u])
    mp = [jnp.concatenate(mp[u], axis=1) for u in U]
    bw = [jnp.concatenate(bw[u], axis=1) for u in U]
    xm = [jnp.concatenate(xm[u], axis=0) for u in U]

    st_old = [st_ref[gs[u]] for u in U]
    y = [jnp.dot(mp[u], xm[u], preferred_element_type=F32) for u in U]
    yo = [jnp.dot(cm_b[u], st_old[u].astype(BF16), preferred_element_type=F32) for u in U]
    sn = [jnp.dot(bw[u], xm[u], preferred_element_type=F32) for u in U]
    for u in U:
        st_ref[gs[u]] = st_old[u] * dch[u] + sn[u]
    for u in U:
        cols = slice(u * gw, (u + 1) * gw)
        yy = y[u] + yo[u] * dfs[u] + dskip_ref[:, cols] * xs_b[u].astype(F32)
        hg = yy * z_ref[0, :, cols].astype(F32)
        y_ref[0, :, cols] = (_rms(hg, nw_ref[:, cols])).astype(BF16)


def _ssd(xbc3, dt3, z3, dtb, alog, dskip, nw, tri, ltri, *, ssm_w):
    batch, seq, xbc_w = xbc3.shape
    L = SSD_CHUNK
    N = SSM_STATE
    G = SSM_GROUPS
    gps = SSD_GROUPS_PER_STEP
    assert seq % L == 0 and G % gps == 0
    nc = seq // L
    gw = ssm_w // G
    hpg = gw // SSM_HEAD_DIM
    assert hpg <= 8 and xbc_w == ssm_w + 2 * G * N and gw % LANES == 0
    b_off = ssm_w // (gps * N)
    c_off = b_off + G // gps

    return pl.pallas_call(
        functools.partial(_ssd_kernel, hpg=hpg, gps=gps),
        grid=(batch, nc, G // gps),
        in_specs=[
            pl.BlockSpec((1, L, gps * gw), lambda b, c, g: (b, c, g)),
            pl.BlockSpec((1, L, gps * N), lambda b, c, g: (b, c, b_off + g)),
            pl.BlockSpec((1, L, gps * N), lambda b, c, g: (b, c, c_off + g)),
            pl.BlockSpec((1, L, LANES), lambda b, c, g: (b, c, 0)),
            pl.BlockSpec((1, LANES), lambda b, c, g: (0, 0)),
            pl.BlockSpec((1, LANES), lambda b, c, g: (0, 0)),
            pl.BlockSpec((1, gps * gw), lambda b, c, g: (0, g)),
            pl.BlockSpec((1, L, gps * gw), lambda b, c, g: (b, c, g)),
            pl.BlockSpec((1, gps * gw), lambda b, c, g: (0, g)),
            pl.BlockSpec((L, L), lambda b, c, g: (0, 0)),
            pl.BlockSpec((L, L), lambda b, c, g: (0, 0)),
        ],
        out_specs=pl.BlockSpec((1, L, gps * gw), lambda b, c, g: (b, c, g)),
        out_shape=jax.ShapeDtypeStruct((batch, seq, ssm_w), BF16),
        scratch_shapes=[
            pltpu.VMEM((G, N, gw), F32),
            pltpu.VMEM((G, L, LANES), F32),
            pltpu.VMEM((G, 8, L), F32),
            pltpu.VMEM((LANES, L), F32),
        ],
        compiler_params=_cparams("parallel", "arbitrary", "arbitrary"),
        name="ssd",
    )(xbc3, xbc3, xbc3, dt3, dtb, alog, dskip, z3, nw, tri, ltri)


def _proj_out_kernel(attn_ref, y_ref, an_ref, w_ref, x_ref, o_ref, cat_ref, *, n_heads):
    j = pl.program_id(1)
    aw = n_heads * ATTN_HEAD_DIM

    @pl.when(j == 0)
    def _():
        a = jnp.concatenate([attn_ref[0, hh].astype(F32) for hh in range(n_heads)], axis=1)
        cat_ref[:, 0:aw] = _rms(a, an_ref[...]).astype(BF16)
        cat_ref[:, aw:] = y_ref[...]

    tm = x_ref.shape[0]
    rc = min(tm, ROW_CHUNK)
    starts = range(0, tm, rc)
    accs = [jnp.dot(cat_ref[r0:r0 + rc, :], w_ref[...], preferred_element_type=F32) for r0 in starts]
    for r0, acc in zip(starts, accs):
        o_ref[r0:r0 + rc, :] = x_ref[r0:r0 + rc, :] + acc


def _proj_out(attn, y2, an, w_out, x2, *, seq):
    batch, n_heads, _, dh = attn.shape
    T, D = x2.shape
    aw = n_heads * dh
    sw = y2.shape[1]
    tm = _row_tile(seq, 1024)
    tn = 512
    nsb = seq // tm
    return pl.pallas_call(
        functools.partial(_proj_out_kernel, n_heads=n_heads),
        grid=(T // tm, D // tn),
        in_specs=[
            pl.BlockSpec((1, n_heads, tm, dh), lambda i, j: (i // nsb, 0, i % nsb, 0)),
            pl.BlockSpec((tm, sw), lambda i, j: (i, 0)),
            pl.BlockSpec((1, aw), lambda i, j: (0, 0)),
            pl.BlockSpec((aw + sw, tn), lambda i, j: (0, j)),
            pl.BlockSpec((tm, tn), lambda i, j: (i, j)),
        ],
        out_specs=pl.BlockSpec((tm, tn), lambda i, j: (i, j)),
        out_shape=jax.ShapeDtypeStruct((T, D), F32),
        scratch_shapes=[pltpu.VMEM((tm, aw + sw), BF16)],
        compiler_params=_cparams("parallel", "arbitrary"),
        name="proj_out",
    )(attn, y2, an, w_out, x2)


def _ffn_kernel(x_ref, ln_ref, wg_ref, wv_ref, cwg_ref, cwv_ref, cbg_ref, cbv_ref, wd_ref, fn_ref, o_ref,
                h_ref, halo_ref, ug_ref, uv_ref, *, nsb, final):
    i = pl.program_id(0)
    j = pl.program_id(1)
    tm = x_ref.shape[0]

    @pl.when((i == 0) & (j == 0))
    def _():
        halo_ref[...] = jnp.zeros_like(halo_ref)

    @pl.when(j == 0)
    def _():
        x = x_ref[...]
        h_ref[...] = _rms(x, ln_ref[...]).astype(BF16)
        o_ref[...] = x

    first = i % nsb == 0
    rc = min(tm, FFN_ROW_CHUNK)
    starts = range(0, tm, rc)
    ups = [(jnp.dot(h_ref[r0:r0 + rc, :], wg_ref[...], preferred_element_type=F32),
            jnp.dot(h_ref[r0:r0 + rc, :], wv_ref[...], preferred_element_type=F32)) for r0 in starts]
    ug_ref[0:8, :] = jnp.where(first, 0.0, halo_ref[j, 0])
    uv_ref[0:8, :] = jnp.where(first, 0.0, halo_ref[j, 1])
    halo_ref[j, 0] = ups[-1][0][rc - 8:, :]
    halo_ref[j, 1] = ups[-1][1][rc - 8:, :]

    def conv(u_ref, cw_ref, cb_ref, r0):
        out = cb_ref[...]
        for t in range(FFN_CONV):
            o = 8 - (FFN_CONV - 1) + t + r0
            out = out + cw_ref[t:t + 1, :] * u_ref[o:o + rc, :]
        return out

    for r0, (g, v) in zip(starts, ups):
        ug_ref[8 + r0:8 + r0 + rc, :] = g
        uv_ref[8 + r0:8 + r0 + rc, :] = v
        act = (_silu(conv(ug_ref, cwg_ref, cbg_ref, r0)) * conv(uv_ref, cwv_ref, cbv_ref, r0)).astype(BF16)
        o_ref[r0:r0 + rc, :] += jnp.dot(act, wd_ref[...], preferred_element_type=F32)

    if final:
        @pl.when(j == pl.num_programs(1) - 1)
        def _():
            o_ref[...] = _rms(o_ref[...], fn_ref[...])


def _ffn(x2, ln, w_up, conv_w, conv_b, w_down, fn, *, seq, final):
    T, D = x2.shape
    dff = w_down.shape[0]
    tm = _row_tile(seq, 512)
    tf = 512
    assert dff % tf == 0 and w_up.shape == (D, 2 * dff)
    nsb = seq // tm
    nf = dff // tf
    return pl.pallas_call(
        functools.partial(_ffn_kernel, nsb=nsb, final=final),
        grid=(T // tm, nf),
        in_specs=[
            pl.BlockSpec((tm, D), lambda i, j: (i, 0)),
            pl.BlockSpec((1, D), lambda i, j: (0, 0)),
            pl.BlockSpec((D, tf), lambda i, j: (0, j)),
            pl.BlockSpec((D, tf), lambda i, j: (0, nf + j)),
            pl.BlockSpec((FFN_CONV, tf), lambda i, j: (0, j)),
            pl.BlockSpec((FFN_CONV, tf), lambda i, j: (0, nf + j)),
            pl.BlockSpec((1, tf), lambda i, j: (0, j)),
            pl.BlockSpec((1, tf), lambda i, j: (0, nf + j)),
            pl.BlockSpec((tf, D), lambda i, j: (j, 0)),
            pl.BlockSpec((1, D), lambda i, j: (0, 0)),
        ],
        out_specs=pl.BlockSpec((tm, D), lambda i, j: (i, 0)),
        out_shape=jax.ShapeDtypeStruct((T, D), F32),
        scratch_shapes=[
            pltpu.VMEM((tm, D), BF16),
            pltpu.VMEM((nf, 2, 8, tf), F32),
            pltpu.VMEM((tm + 8, tf), F32),
            pltpu.VMEM((tm + 8, tf), F32),
        ],
        compiler_params=_cparams("arbitrary", "arbitrary"),
        name="ffn_final" if final else "ffn",
    )(x2, ln, w_up, w_up, conv_w, conv_w, conv_b, conv_b, w_down, fn)


def _rope_tables(seq):
    half = ATTN_HEAD_DIM // 2
    inv_freq = jnp.power(ROPE_THETA, -jnp.arange(half, dtype=F32) / half)
    ang = jnp.arange(seq, dtype=F32)[:, None] * inv_freq[None, :]
    cos, sin = jnp.cos(ang), jnp.sin(ang)
    return jnp.concatenate([cos, cos], axis=-1), jnp.concatenate([-sin, sin], axis=-1)


def _chunk_constants():
    L = SSD_CHUNK
    low = np.tril(np.ones((L, L), np.float32))
    tri = jnp.asarray(np.where(low > 0, 0.0, NEG_INF).astype(np.float32))
    ltri = jnp.asarray(low).astype(BF16)
    return tri, ltri


def _block_onehot(seq):
    oh = np.zeros((seq, LANES), np.float32)
    oh[np.arange(seq), np.arange(seq) // MOBA_BLOCK] = 1.0
    return jnp.asarray(oh).astype(BF16)


def _pad_lanes(v):
    return jnp.pad(v, (0, LANES - v.shape[0]))[None, :]


def kernel(x, ln1, w_in, attn_norm, ssm_conv_w, ssm_conv_b, dt_bias, a_log, d_skip, ssm_norm, w_out, ln2, w_up, ffn_conv_w, ffn_conv_b, w_down, final_norm):
    batch, seq, d_model = x.shape
    depth = ln1.shape[0]
    attn_w = attn_norm.shape[1]
    ssm_w = ssm_norm.shape[1]
    xbc_w = ssm_conv_w.shape[2]
    n_ssm_heads = a_log.shape[1]
    assert ssm_w // n_ssm_heads == SSM_HEAD_DIM and n_ssm_heads <= LANES
    main_w = 3 * attn_w + ssm_w + xbc_w
    cos_t, sin_t = _rope_tables(seq)
    tri, ltri = _chunk_constants()
    onehot = _block_onehot(seq)

    x2 = x.reshape(batch * seq, d_model)
    for i in range(depth):
        w_main = w_in[i].astype(BF16)
        w_dt = jnp.pad(w_main[:, main_w:], ((0, 0), (0, LANES - n_ssm_heads)))
        q, k, v, z2, xbc2, dt2 = _proj_in(
            x2, ln1[i][None, :], w_main, w_dt, cos_t, sin_t, ssm_conv_w[i], ssm_conv_b[i][None, :],
            batch=batch, seq=seq, attn_w=attn_w, ssm_w=ssm_w, xbc_w=xbc_w)

        attn = _moba(q, k, v, onehot)

        y3 = _ssd(
            xbc2.reshape(batch, seq, xbc_w), dt2.reshape(batch, seq, LANES),
            z2.reshape(batch, seq, ssm_w),
            _pad_lanes(dt_bias[i]), _pad_lanes(a_log[i]),
            jnp.repeat(d_skip[i], SSM_HEAD_DIM)[None, :], ssm_norm[i][None, :], tri, ltri,
            ssm_w=ssm_w)

        x2 = _proj_out(attn, y3.reshape(batch * seq, ssm_w), attn_norm[i][None, :],
                       w_out[i].astype(BF16), x2, seq=seq)

        x2 = _ffn(x2, ln2[i][None, :], w_up[i].astype(BF16), ffn_conv_w[i], ffn_conv_b[i][None, :],
                  w_down[i].astype(BF16), final_norm[None, :], seq=seq, final=(i == depth - 1))
    return x2.reshape(batch, seq, d_model)
```

```python
import functools

import numpy as np
import jax
import jax.numpy as jnp
from jax import lax
from jax.experimental import pallas as pl
from jax.experimental.pallas import tpu as pltpu

F32 = jnp.float32
BF16 = jnp.bfloat16

NORM_EPS = 1e-6
NEG_INF = -1e30
LOG2E = 1.4426950408889634
ROPE_THETA = 10000.0

ATTN_HEAD_DIM = 128
MOBA_BLOCK = 256
MOBA_TOPK = 3
MOBA_MAX_BLOCKS = 8

SSM_HEAD_DIM = 64
SSM_GROUPS = 8
SSM_STATE = 128
SSM_CONV = 4
SSD_CHUNK = 256
SSD_GROUPS_PER_STEP = 8
FFN_CONV = 3
ROW_CHUNK = 512
FFN_ROW_CHUNK = 256

LANES = 128
BF16_ROWS = 16
VMEM_LIMIT_BYTES = 56 * 1024 * 1024


def _cparams(*sem):
    return pltpu.CompilerParams(dimension_semantics=sem, vmem_limit_bytes=VMEM_LIMIT_BYTES)


def _row_tile(seq, pref):
    t = min(seq, pref)
    assert seq % t == 0
    return t


def _rms(xf, g):
    ms = jnp.mean(xf * xf, axis=-1, keepdims=True)
    return xf * lax.rsqrt(ms + NORM_EPS) * g


def _silu(x):
    return x * (1.0 / (1.0 + jnp.exp(-x)))


def _nt_dot(a, b):
    return lax.dot_general(a, b, (((1,), (1,)), ((), ())), preferred_element_type=F32)


def _proj_in_kernel(x_ref, ln_ref, w_ref, wdt_ref, cos_ref, sin_ref, cw_ref, cb_ref,
                    q_ref, k_ref, v_ref, z_ref, xbc_ref, dt_ref,
                    h_ref, halo_ref, *, nq, nz, hpt, nsb):
    i = pl.program_id(0)
    j = pl.program_id(1)
    tm = x_ref.shape[0]

    @pl.when(j == 0)
    def _():
        h = _rms(x_ref[...], ln_ref[...]).astype(BF16)
        h_ref[...] = h
        dt_ref[...] = jnp.dot(h, wdt_ref[...], preferred_element_type=F32)

    rc = min(tm, ROW_CHUNK)
    starts = range(0, tm, rc)

    def mm_chunks():
        return [jnp.dot(h_ref[r0:r0 + rc, :], w_ref[...], preferred_element_type=F32) for r0 in starts]

    def rope_store(o_ref):
        accs = mm_chunks()
        for r0, acc in zip(starts, accs):
            cos = cos_ref[r0:r0 + rc, :]
            sin = sin_ref[r0:r0 + rc, :]
            for hh in range(hpt):
                a = acc[:, hh * LANES:(hh + 1) * LANES]
                o_ref[0, hh, r0:r0 + rc, :] = (
                    a * cos + pltpu.roll(a, ATTN_HEAD_DIM // 2, axis=1) * sin).astype(BF16)

    @pl.when(j < nq)
    def _():
        rope_store(q_ref)

    @pl.when((j >= nq) & (j < 2 * nq))
    def _():
        rope_store(k_ref)

    @pl.when((j >= 2 * nq) & (j < 3 * nq))
    def _():
        accs = mm_chunks()
        for r0, acc in zip(starts, accs):
            for hh in range(hpt):
                v_ref[0, hh, r0:r0 + rc, :] = acc[:, hh * LANES:(hh + 1) * LANES].astype(BF16)

    @pl.when((j >= 3 * nq) & (j < 3 * nq + nz))
    def _():
        accs = mm_chunks()
        for r0, acc in zip(starts, accs):
            z_ref[r0:r0 + rc, :] = _silu(acc).astype(BF16)

    @pl.when((i == 0) & (j == 0))
    def _():
        halo_ref[...] = jnp.zeros_like(halo_ref)

    @pl.when(j >= 3 * nq + nz)
    def _():
        jx = j - (3 * nq + nz)
        prev = jnp.where(i % nsb == 0, 0.0, halo_ref[jx])
        cw = cw_ref[...]
        row = lax.broadcasted_iota(jnp.int32, prev.shape, 0)
        accs = mm_chunks()
        for r0, acc in zip(starts, accs):
            conv = cb_ref[...] + cw[SSM_CONV - 1:SSM_CONV, :] * acc
            for sh in range(1, SSM_CONV):
                r = pltpu.roll(acc, sh, axis=0)
                head = jnp.where(row < sh, pltpu.roll(prev, sh, axis=0), r[0:8])
                xk = jnp.concatenate([head, r[8:]], axis=0)
                conv = conv + cw[SSM_CONV - 1 - sh:SSM_CONV - sh, :] * xk
            xbc_ref[r0:r0 + rc, :] = _silu(conv).astype(BF16)
            prev = acc[rc - 8:, :]
        halo_ref[jx] = prev


def _proj_in(x2, ln, w_main, w_dt, cos_t, sin_t, conv_w, conv_b, *, batch, seq, attn_w, ssm_w, xbc_w):
    T, D = x2.shape
    tm = _row_tile(seq, 1024)
    tn = 512
    nsb = seq // tm
    n_heads = attn_w // ATTN_HEAD_DIM
    hpt = tn // ATTN_HEAD_DIM
    nq = attn_w // tn
    nz = ssm_w // tn
    nx = xbc_w // tn
    nj = 3 * nq + nz + nx
    assert w_main.shape[0] == D and w_main.shape[1] >= nj * tn

    def clampj(lo, n):
        return lambda j: jnp.clip(j - lo, 0, n - 1)

    qj, kj, vj = clampj(0, nq), clampj(nq, nq), clampj(2 * nq, nq)
    zj, xj = clampj(3 * nq, nz), clampj(3 * nq + nz, nx)

    head_shape = jax.ShapeDtypeStruct((batch, n_heads, seq, ATTN_HEAD_DIM), BF16)

    def head_spec(fj):
        return pl.BlockSpec((1, hpt, tm, ATTN_HEAD_DIM), lambda i, j: (i // nsb, fj(j), i % nsb, 0))

    return pl.pallas_call(
        functools.partial(_proj_in_kernel, nq=nq, nz=nz, hpt=hpt, nsb=nsb),
        grid=(T // tm, nj),
        in_specs=[
            pl.BlockSpec((tm, D), lambda i, j: (i, 0)),
            pl.BlockSpec((1, D), lambda i, j: (0, 0)),
            pl.BlockSpec((D, tn), lambda i, j: (0, j)),
            pl.BlockSpec((D, LANES), lambda i, j: (0, 0)),
            pl.BlockSpec((tm, ATTN_HEAD_DIM), lambda i, j: (i % nsb, 0)),
            pl.BlockSpec((tm, ATTN_HEAD_DIM), lambda i, j: (i % nsb, 0)),
            pl.BlockSpec((SSM_CONV, tn), lambda i, j: (0, xj(j))),
            pl.BlockSpec((1, tn), lambda i, j: (0, xj(j))),
        ],
        out_specs=[
            head_spec(qj), head_spec(kj), head_spec(vj),
            pl.BlockSpec((tm, tn), lambda i, j: (i, zj(j))),
            pl.BlockSpec((tm, tn), lambda i, j: (i, xj(j))),
            pl.BlockSpec((tm, LANES), lambda i, j: (i, 0)),
        ],
        out_shape=[
            head_shape, head_shape, head_shape,
            jax.ShapeDtypeStruct((T, ssm_w), BF16),
            jax.ShapeDtypeStruct((T, xbc_w), BF16),
            jax.ShapeDtypeStruct((T, LANES), F32),
        ],
        scratch_shapes=[
            pltpu.VMEM((tm, D), BF16),
            pltpu.VMEM((nx, 8, tn), F32),
        ],
        compiler_params=_cparams("arbitrary", "arbitrary"),
        name="proj_in",
    )(x2, ln, w_main, w_dt, cos_t, sin_t, conv_w, conv_b)


def _moba_kernel(q_ref, k_ref, v_ref, oh_ref, o_ref, qa_ref, *, nb, scale):
    seq = nb * MOBA_BLOCK
    nbp = MOBA_MAX_BLOCKS
    blk = MOBA_BLOCK

    kf = k_ref[0, 0].astype(F32)
    rows = [jnp.sum(kf[b * blk:(b + 1) * blk], axis=0, keepdims=True) for b in range(nb)]
    if nb < nbp:
        rows.append(jnp.zeros((nbp - nb, ATTN_HEAD_DIM), F32))
    kmean = jnp.concatenate(rows, axis=0) * (1.0 / blk)
    k_hi = kmean.astype(BF16)
    k_lo = (kmean - k_hi.astype(F32)).astype(BF16)
    q = q_ref[0, 0]
    g2 = _nt_dot(jnp.concatenate([k_hi, k_lo], axis=0), q)
    gate = g2[0:nbp] + g2[nbp:2 * nbp]
    own = jnp.right_shift(lax.broadcasted_iota(jnp.int32, (nbp, seq), 1), blk.bit_length() - 1)
    kb = lax.broadcasted_iota(jnp.int32, (nbp, seq), 0)
    rank = jnp.zeros((nbp, seq), F32)
    for b in range(nb):
        gb = gate[b:b + 1, :]
        beats = (b < own) & ((gb > gate) | ((gb == gate) & (b < kb)))
        rank = rank + jnp.where(beats, 1.0, 0.0)
    allowed = ((kb < own) & (rank < MOBA_TOPK)) | (kb == own)
    bias_t = jnp.where(allowed, 0.0, NEG_INF)
    bias_t = jnp.concatenate([bias_t, jnp.zeros((LANES - nbp, seq), F32)], axis=0)
    qa_ref[:, 0:ATTN_HEAD_DIM] = q
    qa_ref[:, ATTN_HEAD_DIM:] = bias_t.T.astype(BF16)

    qi = lax.broadcasted_iota(jnp.int32, (blk, blk), 0)
    ki = lax.broadcasted_iota(jnp.int32, (blk, blk), 1)
    causal = ki <= qi
    c2 = scale * LOG2E
    def scores(i):
        hi = (i + 1) * blk
        ka = jnp.concatenate([k_ref[0, 0, 0:hi, :], oh_ref[0:hi, :]], axis=1)
        return _nt_dot(qa_ref[i * blk:hi, :], ka)

    s_next = scores(0)
    for i in range(nb):
        hi = (i + 1) * blk
        s = s_next
        if i + 1 < nb:
            s_next = scores(i + 1)
        s_own = jnp.where(causal, s[:, i * blk:], NEG_INF)
        s = jnp.concatenate([s[:, 0:i * blk], s_own], axis=1) if i else s_own
        m = jnp.max(s, axis=1, keepdims=True)
        p = jnp.exp2((s - m) * c2)
        l = jnp.sum(p, axis=1, keepdims=True)
        acc = jnp.dot(p.astype(BF16), v_ref[0, 0, 0:hi, :], preferred_element_type=F32)
        o_ref[0, 0, i * blk:hi, :] = (acc / l).astype(BF16)


def _moba(q, k, v, onehot):
    batch, n_heads, seq, dh = q.shape
    nb = seq // MOBA_BLOCK
    assert seq % MOBA_BLOCK == 0 and nb <= MOBA_MAX_BLOCKS and dh == ATTN_HEAD_DIM
    full = pl.BlockSpec((1, 1, seq, dh), lambda b, h: (b, h, 0, 0))
    return pl.pallas_call(
        functools.partial(_moba_kernel, nb=nb, scale=dh ** -0.5),
        grid=(batch, n_heads),
        in_specs=[full, full, full, pl.BlockSpec((seq, LANES), lambda b, h: (0, 0))],
        out_specs=full,
        out_shape=jax.ShapeDtypeStruct((batch, n_heads, seq, dh), BF16),
        scratch_shapes=[pltpu.VMEM((seq, 2 * dh), BF16)],
        compiler_params=_cparams("parallel", "parallel"),
        name="moba",
    )(q, k, v, onehot)


def _ssd_kernel(xx_ref, xb_ref, xc_ref, dt_ref, dtb_ref, alog_ref, dskip_ref, z_ref, nw_ref,
                tri_ref, ltri_ref, y_ref,
                st_ref, acsc_ref, rows_ref, rowt_ref, *, hpg, gps):
    c = pl.program_id(1)
    gp = pl.program_id(2)
    L = SSD_CHUNK
    P = SSM_HEAD_DIM
    N = SSM_STATE
    gw = hpg * P
    n_groups = st_ref.shape[0]

    @pl.when(gp == 0)
    def _():
        dtv = dt_ref[0] + dtb_ref[...]
        dtv = jnp.maximum(dtv, 0.0) + jnp.log1p(jnp.exp(-jnp.abs(dtv)))
        a = dtv * (-jnp.exp(alog_ref[...]))
        a_hi = a.astype(BF16)
        r1 = a - a_hi.astype(F32)
        a_mid = r1.astype(BF16)
        a_lo = (r1 - a_mid.astype(F32)).astype(BF16)
        cs3 = jnp.dot(ltri_ref[...], jnp.concatenate([a_hi, a_mid, a_lo], axis=1),
                      preferred_element_type=F32)
        acs2 = (cs3[:, 0:LANES] + cs3[:, LANES:2 * LANES] + cs3[:, 2 * LANES:]) * LOG2E
        rowt_ref[...] = (acs2 - jnp.log2(dtv)).T
        for gg in range(n_groups):
            sh = (LANES - hpg * gg) % LANES
            acsc_ref[gg] = pltpu.roll(acs2, sh, axis=1) if sh else acs2
            rows_ref[gg, 0:hpg, :] = rowt_ref[hpg * gg:hpg * (gg + 1), :]

    @pl.when(c == 0)
    def _():
        for u in range(gps):
            st_ref[gp * gps + u] = jnp.zeros((N, gw), F32)

    tri = tri_ref[...]
    lane_head = jnp.right_shift(lax.broadcasted_iota(jnp.int32, (1, gw), 1), P.bit_length() - 1)

    U = range(gps)
    gs = [gp * gps + u for u in U]
    xs_b = [xx_ref[0, :, u * gw:(u + 1) * gw] for u in U]
    bm_b = [xb_ref[0, :, u * N:(u + 1) * N] for u in U]
    cm_b = [xc_ref[0, :, u * N:(u + 1) * N] for u in U]
    cb_mat = [_nt_dot(cm_b[u], bm_b[u]) for u in U]
    bm_t = [bm_b[u].astype(F32).T for u in U]
    acsc = [acsc_ref[gs[u]] for u in U]
    e_col = [jnp.exp2(acsc[u]) for u in U]
    a_end = [acsc[u][L - 1:L, :] for u in U]

    mp = [[] for _ in U]
    bw = [[] for _ in U]
    xm = [[] for _ in U]
    dfs = [jnp.zeros((L, gw), F32) for _ in U]
    dch = [jnp.zeros((1, gw), F32) for _ in U]
    for r in range(hpg):
        hmask = lane_head == r
        for u in U:
            a_col = acsc[u][:, r:r + 1]
            a_row = rows_ref[gs[u], r:r + 1, :]
            a_last = a_end[u][:, r:r + 1]
            mp[u].append((cb_mat[u] * jnp.exp2(a_col - a_row + tri)).astype(BF16))
            bw[u].append((bm_t[u] * jnp.exp2(a_last - a_row)).astype(BF16))
            xm[u].append(jnp.where(hmask, xs_b[u], jnp.zeros_like(xs_b[u])))
            dfs[u] = jnp.where(hmask, jnp.broadcast_to(e_col[u][:, r:r + 1], (L, gw)), dfs[u])
            dch[u] = jnp.where(hmask, jnp.broadcast_to(jnp.exp2(a_last), (1, gw)), dch[u])
    mp = [jnp.concatenate(mp[u], axis=1) for u in U]
    bw = [jnp.concatenate(bw[u], axis=1) for u in U]
    xm = [jnp.concatenate(xm[u], axis=0) for u in U]

    st_old = [st_ref[gs[u]] for u in U]
    y = [jnp.dot(mp[u], xm[u], preferred_element_type=F32) for u in U]
    yo = [jnp.dot(cm_b[u], st_old[u].astype(BF16), preferred_element_type=F32) for u in U]
    sn = [jnp.dot(bw[u], xm[u], preferred_element_type=F32) for u in U]
    for u in U:
        st_ref[gs[u]] = st_old[u] * dch[u] + sn[u]
    for u in U:
        cols = slice(u * gw, (u + 1) * gw)
        yy = y[u] + yo[u] * dfs[u] + dskip_ref[:, cols] * xs_b[u].astype(F32)
        hg = yy * z_ref[0, :, cols].astype(F32)
        y_ref[0, :, cols] = (_rms(hg, nw_ref[:, cols])).astype(BF16)


def _ssd(xbc3, dt3, z3, dtb, alog, dskip, nw, tri, ltri, *, ssm_w):
    batch, seq, xbc_w = xbc3.shape
    L = SSD_CHUNK
    N = SSM_STATE
    G = SSM_GROUPS
    gps = SSD_GROUPS_PER_STEP
    assert seq % L == 0 and G % gps == 0
    nc = seq // L
    gw = ssm_w // G
    hpg = gw // SSM_HEAD_DIM
    assert hpg <= 8 and xbc_w == ssm_w + 2 * G * N and gw % LANES == 0
    b_off = ssm_w // (gps * N)
    c_off = b_off + G // gps

    return pl.pallas_call(
        functools.partial(_ssd_kernel, hpg=hpg, gps=gps),
        grid=(batch, nc, G // gps),
        in_specs=[
            pl.BlockSpec((1, L, gps * gw), lambda b, c, g: (b, c, g)),
            pl.BlockSpec((1, L, gps * N), lambda b, c, g: (b, c, b_off + g)),
            pl.BlockSpec((1, L, gps * N), lambda b, c, g: (b, c, c_off + g)),
            pl.BlockSpec((1, L, LANES), lambda b, c, g: (b, c, 0)),
            pl.BlockSpec((1, LANES), lambda b, c, g: (0, 0)),
            pl.BlockSpec((1, LANES), lambda b, c, g: (0, 0)),
            pl.BlockSpec((1, gps * gw), lambda b, c, g: (0, g)),
            pl.BlockSpec((1, L, gps * gw), lambda b, c, g: (b, c, g)),
            pl.BlockSpec((1, gps * gw), lambda b, c, g: (0, g)),
            pl.BlockSpec((L, L), lambda b, c, g: (0, 0)),
            pl.BlockSpec((L, L), lambda b, c, g: (0, 0)),
        ],
        out_specs=pl.BlockSpec((1, L, gps * gw), lambda b, c, g: (b, c, g)),
        out_shape=jax.ShapeDtypeStruct((batch, seq, ssm_w), BF16),
        scratch_shapes=[
            pltpu.VMEM((G, N, gw), F32),
            pltpu.VMEM((G, L, LANES), F32),
            pltpu.VMEM((G, 8, L), F32),
            pltpu.VMEM((LANES, L), F32),
        ],
        compiler_params=_cparams("parallel", "arbitrary", "arbitrary"),
        name="ssd",
    )(xbc3, xbc3, xbc3, dt3, dtb, alog, dskip, z3, nw, tri, ltri)


def _proj_out_kernel(attn_ref, y_ref, an_ref, w_ref, x_ref, o_ref, cat_ref, *, n_heads):
    j = pl.program_id(1)
    aw = n_heads * ATTN_HEAD_DIM

    @pl.when(j == 0)
    def _():
        a = jnp.concatenate([attn_ref[0, hh].astype(F32) for hh in range(n_heads)], axis=1)
        cat_ref[:, 0:aw] = _rms(a, an_ref[...]).astype(BF16)
        cat_ref[:, aw:] = y_ref[...]

    tm = x_ref.shape[0]
    rc = min(tm, ROW_CHUNK)
    starts = range(0, tm, rc)
    accs = [jnp.dot(cat_ref[r0:r0 + rc, :], w_ref[...], preferred_element_type=F32) for r0 in starts]
    for r0, acc in zip(starts, accs):
        o_ref[r0:r0 + rc, :] = x_ref[r0:r0 + rc, :] + acc


def _proj_out(attn, y2, an, w_out, x2, *, seq):
    batch, n_heads, _, dh = attn.shape
    T, D = x2.shape
    aw = n_heads * dh
    sw = y2.shape[1]
    tm = _row_tile(seq, 1024)
    tn = 512
    nsb = seq // tm
    return pl.pallas_call(
        functools.partial(_proj_out_kernel, n_heads=n_heads),
        grid=(T // tm, D // tn),
        in_specs=[
            pl.BlockSpec((1, n_heads, tm, dh), lambda i, j: (i // nsb, 0, i % nsb, 0)),
            pl.BlockSpec((tm, sw), lambda i, j: (i, 0)),
            pl.BlockSpec((1, aw), lambda i, j: (0, 0)),
            pl.BlockSpec((aw + sw, tn), lambda i, j: (0, j)),
            pl.BlockSpec((tm, tn), lambda i, j: (i, j)),
        ],
        out_specs=pl.BlockSpec((tm, tn), lambda i, j: (i, j)),
        out_shape=jax.ShapeDtypeStruct((T, D), F32),
        scratch_shapes=[pltpu.VMEM((tm, aw + sw), BF16)],
        compiler_params=_cparams("parallel", "arbitrary"),
        name="proj_out",
    )(attn, y2, an, w_out, x2)


def _ffn_kernel(x_ref, ln_ref, wg_ref, wv_ref, cwg_ref, cwv_ref, cbg_ref, cbv_ref, wd_ref, fn_ref, o_ref,
                h_ref, halo_ref, ug_ref, uv_ref, *, nsb, final):
    i = pl.program_id(0)
    j = pl.program_id(1)
    tm = x_ref.shape[0]

    @pl.when((i == 0) & (j == 0))
    def _():
        halo_ref[...] = jnp.zeros_like(halo_ref)

    @pl.when(j == 0)
    def _():
        x = x_ref[...]
        h_ref[...] = _rms(x, ln_ref[...]).astype(BF16)
        o_ref[...] = x

    first = i % nsb == 0
    rc = min(tm, FFN_ROW_CHUNK)
    starts = range(0, tm, rc)
    ups = [(jnp.dot(h_ref[r0:r0 + rc, :], wg_ref[...], preferred_element_type=F32),
            jnp.dot(h_ref[r0:r0 + rc, :], wv_ref[...], preferred_element_type=F32)) for r0 in starts]
    ug_ref[0:8, :] = jnp.where(first, 0.0, halo_ref[j, 0])
    uv_ref[0:8, :] = jnp.where(first, 0.0, halo_ref[j, 1])
    halo_ref[j, 0] = ups[-1][0][rc - 8:, :]
    halo_ref[j, 1] = ups[-1][1][rc - 8:, :]

    def conv(u_ref, cw_ref, cb_ref, r0):
        out = cb_ref[...]
        for t in range(FFN_CONV):
            o = 8 - (FFN_CONV - 1) + t + r0
            out = out + cw_ref[t:t + 1, :] * u_ref[o:o + rc, :]
        return out

    for r0, (g, v) in zip(starts, ups):
        ug_ref[8 + r0:8 + r0 + rc, :] = g
        uv_ref[8 + r0:8 + r0 + rc, :] = v
        act = (_silu(conv(ug_ref, cwg_ref, cbg_ref, r0)) * conv(uv_ref, cwv_ref, cbv_ref, r0)).astype(BF16)
        o_ref[r0:r0 + rc, :] += jnp.dot(act, wd_ref[...], preferred_element_type=F32)

    if final:
        @pl.when(j == pl.num_programs(1) - 1)
        def _():
            o_ref[...] = _rms(o_ref[...], fn_ref[...])


def _ffn(x2, ln, w_up, conv_w, conv_b, w_down, fn, *, seq, final):
    T, D = x2.shape
    dff = w_down.shape[0]
    tm = _row_tile(seq, 512)
    tf = 512
    assert dff % tf == 0 and w_up.shape == (D, 2 * dff)
    nsb = seq // tm
    nf = dff // tf
    return pl.pallas_call(
        functools.partial(_ffn_kernel, nsb=nsb, final=final),
        grid=(T // tm, nf),
        in_specs=[
            pl.BlockSpec((tm, D), lambda i, j: (i, 0)),
            pl.BlockSpec((1, D), lambda i, j: (0, 0)),
            pl.BlockSpec((D, tf), lambda i, j: (0, j)),
            pl.BlockSpec((D, tf), lambda i, j: (0, nf + j)),
            pl.BlockSpec((FFN_CONV, tf), lambda i, j: (0, j)),
            pl.BlockSpec((FFN_CONV, tf), lambda i, j: (0, nf + j)),
            pl.BlockSpec((1, tf), lambda i, j: (0, j)),
            pl.BlockSpec((1, tf), lambda i, j: (0, nf + j)),
            pl.BlockSpec((tf, D), lambda i, j: (j, 0)),
            pl.BlockSpec((1, D), lambda i, j: (0, 0)),
        ],
        out_specs=pl.BlockSpec((tm, D), lambda i, j: (i, 0)),
        out_shape=jax.ShapeDtypeStruct((T, D), F32),
        scratch_shapes=[
            pltpu.VMEM((tm, D), BF16),
            pltpu.VMEM((nf, 2, 8, tf), F32),
            pltpu.VMEM((tm + 8, tf), F32),
            pltpu.VMEM((tm + 8, tf), F32),
        ],
        compiler_params=_cparams("arbitrary", "arbitrary"),
        name="ffn_final" if final else "ffn",
    )(x2, ln, w_up, w_up, conv_w, conv_w, conv_b, conv_b, w_down, fn)


def _rope_tables(seq):
    half = ATTN_HEAD_DIM // 2
    inv_freq = jnp.power(ROPE_THETA, -jnp.arange(half, dtype=F32) / half)
    ang = jnp.arange(seq, dtype=F32)[:, None] * inv_freq[None, :]
    cos, sin = jnp.cos(ang), jnp.sin(ang)
    return jnp.concatenate([cos, cos], axis=-1), jnp.concatenate([-sin, sin], axis=-1)


def _chunk_constants():
    L = SSD_CHUNK
    low = np.tril(np.ones((L, L), np.float32))
    tri = jnp.asarray(np.where(low > 0, 0.0, NEG_INF).astype(np.float32))
    ltri = jnp.asarray(low).astype(BF16)
    return tri, ltri


def _block_onehot(seq):
    oh = np.zeros((seq, LANES), np.float32)
    oh[np.arange(seq), np.arange(seq) // MOBA_BLOCK] = 1.0
    return jnp.asarray(oh).astype(BF16)


def _pad_lanes(v):
    return jnp.pad(v, (0, LANES - v.shape[0]))[None, :]


def kernel(x, ln1, w_in, attn_norm, ssm_conv_w, ssm_conv_b, dt_bias, a_log, d_skip, ssm_norm, w_out, ln2, w_up, ffn_conv_w, ffn_conv_b, w_down, final_norm):
    batch, seq, d_model = x.shape
    depth = ln1.shape[0]
    attn_w = attn_norm.shape[1]
    ssm_w = ssm_norm.shape[1]
    xbc_w = ssm_conv_w.shape[2]
    n_ssm_heads = a_log.shape[1]
    assert ssm_w // n_ssm_heads == SSM_HEAD_DIM and n_ssm_heads <= LANES
    main_w = 3 * attn_w + ssm_w + xbc_w
    cos_t, sin_t = _rope_tables(seq)
    tri, ltri = _chunk_constants()
    onehot = _block_onehot(seq)

    x2 = x.reshape(batch * seq, d_model)
    for i in range(depth):
        w_main = w_in[i].astype(BF16)
        w_dt = jnp.pad(w_main[:, main_w:], ((0, 0), (0, LANES - n_ssm_heads)))
        q, k, v, z2, xbc2, dt2 = _proj_in(
            x2, ln1[i][None, :], w_main, w_dt, cos_t, sin_t, ssm_conv_w[i], ssm_conv_b[i][None, :],
            batch=batch, seq=seq, attn_w=attn_w, ssm_w=ssm_w, xbc_w=xbc_w)

        attn = _moba(q, k, v, onehot)

        y3 = _ssd(
            xbc2.reshape(batch, seq, xbc_w), dt2.reshape(batch, seq, LANES),
            z2.reshape(batch, seq, ssm_w),
            _pad_lanes(dt_bias[i]), _pad_lanes(a_log[i]),
            jnp.repeat(d_skip[i], SSM_HEAD_DIM)[None, :], ssm_norm[i][None, :], tri, ltri,
            ssm_w=ssm_w)

        x2 = _proj_out(attn, y3.reshape(batch * seq, ssm_w), attn_norm[i][None, :],
                       w_out[i].astype(BF16), x2, seq=seq)

        x2 = _ffn(x2, ln2[i][None, :], w_up[i].astype(BF16), ffn_conv_w[i], ffn_conv_b[i][None, :],
                  w_down[i].astype(BF16), final_norm[None, :], seq=seq, final=(i == depth - 1))
    return x2.reshape(batch, seq, d_model)
```

```python
import functools

import numpy as np
import jax
import jax.numpy as jnp
from jax import lax
from jax.experimental import pallas as pl
from jax.experimental.pallas import tpu as pltpu

F32 = jnp.float32
BF16 = jnp.bfloat16

NORM_EPS = 1e-6
NEG_INF = -1e30
LOG2E = 1.4426950408889634
ROPE_THETA = 10000.0

ATTN_HEAD_DIM = 128
MOBA_BLOCK = 256
MOBA_TOPK = 3
MOBA_MAX_BLOCKS = 8
MOBA_HEADS_PER_STEP = 2

SSM_HEAD_DIM = 64
SSM_GROUPS = 8
SSM_STATE = 128
SSM_CONV = 4
SSD_CHUNK = 256
SSD_GROUPS_PER_STEP = 8
FFN_CONV = 3
ROW_CHUNK = 512
FFN_ROW_CHUNK = 256

LANES = 128
BF16_ROWS = 16
VMEM_LIMIT_BYTES = 56 * 1024 * 1024


def _cparams(*sem):
    return pltpu.CompilerParams(dimension_semantics=sem, vmem_limit_bytes=VMEM_LIMIT_BYTES)


def _row_tile(seq, pref):
    t = min(seq, pref)
    assert seq % t == 0
    return t


def _rms(xf, g):
    ms = jnp.mean(xf * xf, axis=-1, keepdims=True)
    return xf * lax.rsqrt(ms + NORM_EPS) * g


def _silu(x):
    return x * (1.0 / (1.0 + jnp.exp(-x)))


def _nt_dot(a, b):
    return lax.dot_general(a, b, (((1,), (1,)), ((), ())), preferred_element_type=F32)


def _proj_in_kernel(x_ref, ln_ref, w_ref, wdt_ref, cos_ref, sin_ref, cw_ref, cb_ref,
                    q_ref, k_ref, v_ref, z_ref, xbc_ref, dt_ref,
                    h_ref, halo_ref, *, nq, nz, hpt, nsb):
    i = pl.program_id(0)
    j = pl.program_id(1)
    tm = x_ref.shape[0]

    @pl.when(j == 0)
    def _():
        h = _rms(x_ref[...], ln_ref[...]).astype(BF16)
        h_ref[...] = h
        dt_ref[...] = jnp.dot(h, wdt_ref[...], preferred_element_type=F32)

    rc = min(tm, ROW_CHUNK)
    starts = range(0, tm, rc)

    def mm_chunks():
        return [jnp.dot(h_ref[r0:r0 + rc, :], w_ref[...], preferred_element_type=F32) for r0 in starts]

    def rope_store(o_ref):
        accs = mm_chunks()
        for r0, acc in zip(starts, accs):
            cos = cos_ref[r0:r0 + rc, :]
            sin = sin_ref[r0:r0 + rc, :]
            for hh in range(hpt):
                a = acc[:, hh * LANES:(hh + 1) * LANES]
                o_ref[0, hh, r0:r0 + rc, :] = (
                    a * cos + pltpu.roll(a, ATTN_HEAD_DIM // 2, axis=1) * sin).astype(BF16)

    @pl.when(j < nq)
    def _():
        rope_store(q_ref)

    @pl.when((j >= nq) & (j < 2 * nq))
    def _():
        rope_store(k_ref)

    @pl.when((j >= 2 * nq) & (j < 3 * nq))
    def _():
        accs = mm_chunks()
        for r0, acc in zip(starts, accs):
            for hh in range(hpt):
                v_ref[0, hh, r0:r0 + rc, :] = acc[:, hh * LANES:(hh + 1) * LANES].astype(BF16)

    @pl.when((j >= 3 * nq) & (j < 3 * nq + nz))
    def _():
        accs = mm_chunks()
        for r0, acc in zip(starts, accs):
            z_ref[r0:r0 + rc, :] = _silu(acc).astype(BF16)

    @pl.when((i == 0) & (j == 0))
    def _():
        halo_ref[...] = jnp.zeros_like(halo_ref)

    @pl.when(j >= 3 * nq + nz)
    def _():
        jx = j - (3 * nq + nz)
        prev = jnp.where(i % nsb == 0, 0.0, halo_ref[jx])
        cw = cw_ref[...]
        row = lax.broadcasted_iota(jnp.int32, prev.shape, 0)
        accs = mm_chunks()
        for r0, acc in zip(starts, accs):
            conv = cb_ref[...] + cw[SSM_CONV - 1:SSM_CONV, :] * acc
            for sh in range(1, SSM_CONV):
                r = pltpu.roll(acc, sh, axis=0)
                head = jnp.where(row < sh, pltpu.roll(prev, sh, axis=0), r[0:8])
                xk = jnp.concatenate([head, r[8:]], axis=0)
                conv = conv + cw[SSM_CONV - 1 - sh:SSM_CONV - sh, :] * xk
            xbc_ref[r0:r0 + rc, :] = _silu(conv).astype(BF16)
            prev = acc[rc - 8:, :]
        halo_ref[jx] = prev


def _proj_in(x2, ln, w_main, w_dt, cos_t, sin_t, conv_w, conv_b, *, batch, seq, attn_w, ssm_w, xbc_w):
    T, D = x2.shape
    tm = _row_tile(seq, 1024)
    tn = 512
    nsb = seq // tm
    n_heads = attn_w // ATTN_HEAD_DIM
    hpt = tn // ATTN_HEAD_DIM
    nq = attn_w // tn
    nz = ssm_w // tn
    nx = xbc_w // tn
    nj = 3 * nq + nz + nx
    assert w_main.shape[0] == D and w_main.shape[1] >= nj * tn

    def clampj(lo, n):
        return lambda j: jnp.clip(j - lo, 0, n - 1)

    qj, kj, vj = clampj(0, nq), clampj(nq, nq), clampj(2 * nq, nq)
    zj, xj = clampj(3 * nq, nz), clampj(3 * nq + nz, nx)

    head_shape = jax.ShapeDtypeStruct((batch, n_heads, seq, ATTN_HEAD_DIM), BF16)

    def head_spec(fj):
        return pl.BlockSpec((1, hpt, tm, ATTN_HEAD_DIM), lambda i, j: (i // nsb, fj(j), i % nsb, 0))

    return pl.pallas_call(
        functools.partial(_proj_in_kernel, nq=nq, nz=nz, hpt=hpt, nsb=nsb),
        grid=(T // tm, nj),
        in_specs=[
            pl.BlockSpec((tm, D), lambda i, j: (i, 0)),
            pl.BlockSpec((1, D), lambda i, j: (0, 0)),
            pl.BlockSpec((D, tn), lambda i, j: (0, j)),
            pl.BlockSpec((D, LANES), lambda i, j: (0, 0)),
            pl.BlockSpec((tm, ATTN_HEAD_DIM), lambda i, j: (i % nsb, 0)),
            pl.BlockSpec((tm, ATTN_HEAD_DIM), lambda i, j: (i % nsb, 0)),
            pl.BlockSpec((SSM_CONV, tn), lambda i, j: (0, xj(j))),
            pl.BlockSpec((1, tn), lambda i, j: (0, xj(j))),
        ],
        out_specs=[
            head_spec(qj), head_spec(kj), head_spec(vj),
            pl.BlockSpec((tm, tn), lambda i, j: (i, zj(j))),
            pl.BlockSpec((tm, tn), lambda i, j: (i, xj(j))),
            pl.BlockSpec((tm, LANES), lambda i, j: (i, 0)),
        ],
        out_shape=[
            head_shape, head_shape, head_shape,
            jax.ShapeDtypeStruct((T, ssm_w), BF16),
            jax.ShapeDtypeStruct((T, xbc_w), BF16),
            jax.ShapeDtypeStruct((T, LANES), F32),
        ],
        scratch_shapes=[
            pltpu.VMEM((tm, D), BF16),
            pltpu.VMEM((nx, 8, tn), F32),
        ],
        compiler_params=_cparams("arbitrary", "arbitrary"),
        name="proj_in",
    )(x2, ln, w_main, w_dt, cos_t, sin_t, conv_w, conv_b)


def _moba_kernel(q_ref, k_ref, v_ref, oh_ref, o_ref, qa_ref, *, nb, hps, scale):
    seq = nb * MOBA_BLOCK
    nbp = MOBA_MAX_BLOCKS
    blk = MOBA_BLOCK
    H = range(hps)

    own = jnp.right_shift(lax.broadcasted_iota(jnp.int32, (nbp, seq), 1), blk.bit_length() - 1)
    kb = lax.broadcasted_iota(jnp.int32, (nbp, seq), 0)
    for u in H:
        kf = k_ref[0, u].astype(F32)
        rows = [jnp.sum(kf[b * blk:(b + 1) * blk], axis=0, keepdims=True) for b in range(nb)]
        if nb < nbp:
            rows.append(jnp.zeros((nbp - nb, ATTN_HEAD_DIM), F32))
        kmean = jnp.concatenate(rows, axis=0) * (1.0 / blk)
        k_hi = kmean.astype(BF16)
        k_lo = (kmean - k_hi.astype(F32)).astype(BF16)
        q = q_ref[0, u]
        g2 = _nt_dot(jnp.concatenate([k_hi, k_lo], axis=0), q)
        gate = g2[0:nbp] + g2[nbp:2 * nbp]
        rank = jnp.zeros((nbp, seq), F32)
        for b in range(nb):
            gb = gate[b:b + 1, :]
            beats = (b < own) & ((gb > gate) | ((gb == gate) & (b < kb)))
            rank = rank + jnp.where(beats, 1.0, 0.0)
        allowed = ((kb < own) & (rank < MOBA_TOPK)) | (kb == own)
        bias_t = jnp.where(allowed, 0.0, NEG_INF)
        bias_t = jnp.concatenate([bias_t, jnp.zeros((LANES - nbp, seq), F32)], axis=0)
        qa_ref[u, :, 0:ATTN_HEAD_DIM] = q
        qa_ref[u, :, ATTN_HEAD_DIM:] = bias_t.T.astype(BF16)

    qi = lax.broadcasted_iota(jnp.int32, (blk, blk), 0)
    ki = lax.broadcasted_iota(jnp.int32, (blk, blk), 1)
    causal = ki <= qi
    c2 = scale * LOG2E

    def scores(u, i):
        hi = (i + 1) * blk
        ka = jnp.concatenate([k_ref[0, u, 0:hi, :], oh_ref[0:hi, :]], axis=1)
        return _nt_dot(qa_ref[u, i * blk:hi, :], ka)

    s_next = [scores(u, 0) for u in H]
    for i in range(nb):
        hi = (i + 1) * blk
        s_cur = s_next
        if i + 1 < nb:
            s_next = [scores(u, i + 1) for u in H]
        for u in H:
            s = s_cur[u]
            s_own = jnp.where(causal, s[:, i * blk:], NEG_INF)
            s = jnp.concatenate([s[:, 0:i * blk], s_own], axis=1) if i else s_own
            m = jnp.max(s, axis=1, keepdims=True)
            p = jnp.exp2((s - m) * c2)
            l = jnp.sum(p, axis=1, keepdims=True)
            acc = jnp.dot(p.astype(BF16), v_ref[0, u, 0:hi, :], preferred_element_type=F32)
            o_ref[0, u, i * blk:hi, :] = (acc / l).astype(BF16)


def _moba(q, k, v, onehot):
    batch, n_heads, seq, dh = q.shape
    nb = seq // MOBA_BLOCK
    hps = MOBA_HEADS_PER_STEP
    assert seq % MOBA_BLOCK == 0 and nb <= MOBA_MAX_BLOCKS and dh == ATTN_HEAD_DIM and n_heads % hps == 0
    full = pl.BlockSpec((1, hps, seq, dh), lambda b, h: (b, h, 0, 0))
    return pl.pallas_call(
        functools.partial(_moba_kernel, nb=nb, hps=hps, scale=dh ** -0.5),
        grid=(batch, n_heads // hps),
        in_specs=[full, full, full, pl.BlockSpec((seq, LANES), lambda b, h: (0, 0))],
        out_specs=full,
        out_shape=jax.ShapeDtypeStruct((batch, n_heads, seq, dh), BF16),
        scratch_shapes=[pltpu.VMEM((hps, seq, 2 * dh), BF16)],
        compiler_params=_cparams("parallel", "parallel"),
        name="moba",
    )(q, k, v, onehot)


def _ssd_kernel(xx_ref, xb_ref, xc_ref, dt_ref, dtb_ref, alog_ref, dskip_ref, z_ref, nw_ref,
                tri_ref, ltri_ref, y_ref,
                st_ref, acsc_ref, rows_ref, rowt_ref, *, hpg, gps):
    c = pl.program_id(1)
    gp = pl.program_id(2)
    L = SSD_CHUNK
    P = SSM_HEAD_DIM
    N = SSM_STATE
    gw = hpg * P
    n_groups = st_ref.shape[0]

    @pl.when(gp == 0)
    def _():
        dtv = dt_ref[0] + dtb_ref[...]
        dtv = jnp.maximum(dtv, 0.0) + jnp.log1p(jnp.exp(-jnp.abs(dtv)))
        a = dtv * (-jnp.exp(alog_ref[...]))
        a_hi = a.astype(BF16)
        r1 = a - a_hi.astype(F32)
        a_mid = r1.astype(BF16)
        a_lo = (r1 - a_mid.astype(F32)).astype(BF16)
        cs3 = jnp.dot(ltri_ref[...], jnp.concatenate([a_hi, a_mid, a_lo], axis=1),
                      preferred_element_type=F32)
        acs2 = (cs3[:, 0:LANES] + cs3[:, LANES:2 * LANES] + cs3[:, 2 * LANES:]) * LOG2E
        rowt_ref[...] = (acs2 - jnp.log2(dtv)).T
        for gg in range(n_groups):
            sh = (LANES - hpg * gg) % LANES
            acsc_ref[gg] = pltpu.roll(acs2, sh, axis=1) if sh else acs2
            rows_ref[gg, 0:hpg, :] = rowt_ref[hpg * gg:hpg * (gg + 1), :]

    @pl.when(c == 0)
    def _():
        for u in range(gps):
            st_ref[gp * gps + u] = jnp.zeros((N, gw), F32)

    tri = tri_ref[...]
    lane_head = jnp.right_shift(lax.broadcasted_iota(jnp.int32, (1, gw), 1), P.bit_length() - 1)

    U = range(gps)
    gs = [gp * gps + u for u in U]
    xs_b = [xx_ref[0, :, u * gw:(u + 1) * gw] for u in U]
    bm_b = [xb_ref[0, :, u * N:(u + 1) * N] for u in U]
    cm_b = [xc_ref[0, :, u * N:(u + 1) * N] for u in U]
    cb_mat = [_nt_dot(cm_b[u], bm_b[u]) for u in U]
    bm_t = [bm_b[u].astype(F32).T for u in U]
    acsc = [acsc_ref[gs[u]] for u in U]
    e_col = [jnp.exp2(acsc[u]) for u in U]
    a_end = [acsc[u][L - 1:L, :] for u in U]

    mp = [[] for _ in U]
    bw = [[] for _ in U]
    xm = [[] for _ in U]
    dfs = [jnp.zeros((L, gw), F32) for _ in U]
    dch = [jnp.zeros((1, gw), F32) for _ in U]
    for r in range(hpg):
        hmask = lane_head == r
        for u in U:
            a_col = acsc[u][:, r:r + 1]
            a_row = rows_ref[gs[u], r:r + 1, :]
            a_last = a_end[u][:, r:r + 1]
            mp[u].append((cb_mat[u] * jnp.exp2(a_col - a_row + tri)).astype(BF16))
            bw[u].append((bm_t[u] * jnp.exp2(a_last - a_row)).astype(BF16))
            xm[u].append(jnp.where(hmask, xs_b[u], jnp.zeros_like(xs_b[u])))
            dfs[u] = jnp.where(hmask, jnp.broadcast_to(e_col[u][:, r:r + 1], (L, gw)), dfs[u])
            dch[u] = jnp.where(hmask, jnp.broadcast_to(jnp.exp2(a_last), (1, gw)), dch[u])
    mp = [jnp.concatenate(mp[u], axis=1) for u in U]
    bw = [jnp.concatenate(bw[u], axis=1) for u in U]
    xm = [jnp.concatenate(xm[u], axis=0) for u in U]

    st_old = [st_ref[gs[u]] for u in U]
    y = [jnp.dot(mp[u], xm[u], preferred_element_type=F32) for u in U]
    yo = [jnp.dot(cm_b[u], st_old[u].astype(BF16), preferred_element_type=F32) for u in U]
    sn = [jnp.dot(bw[u], xm[u], preferred_element_type=F32) for u in U]
    for u in U:
        st_ref[gs[u]] = st_old[u] * dch[u] + sn[u]
    for u in U:
        cols = slice(u * gw, (u + 1) * gw)
        yy = y[u] + yo[u] * dfs[u] + dskip_ref[:, cols] * xs_b[u].astype(F32)
        hg = yy * z_ref[0, :, cols].astype(F32)
        y_ref[0, :, cols] = (_rms(hg, nw_ref[:, cols])).astype(BF16)


def _ssd(xbc3, dt3, z3, dtb, alog, dskip, nw, tri, ltri, *, ssm_w):
    batch, seq, xbc_w = xbc3.shape
    L = SSD_CHUNK
    N = SSM_STATE
    G = SSM_GROUPS
    gps = SSD_GROUPS_PER_STEP
    assert seq % L == 0 and G % gps == 0
    nc = seq // L
    gw = ssm_w // G
    hpg = gw // SSM_HEAD_DIM
    assert hpg <= 8 and xbc_w == ssm_w + 2 * G * N and gw % LANES == 0
    b_off = ssm_w // (gps * N)
    c_off = b_off + G // gps

    return pl.pallas_call(
        functools.partial(_ssd_kernel, hpg=hpg, gps=gps),
        grid=(batch, nc, G // gps),
        in_specs=[
            pl.BlockSpec((1, L, gps * gw), lambda b, c, g: (b, c, g)),
            pl.BlockSpec((1, L, gps * N), lambda b, c, g: (b, c, b_off + g)),
            pl.BlockSpec((1, L, gps * N), lambda b, c, g: (b, c, c_off + g)),
            pl.BlockSpec((1, L, LANES), lambda b, c, g: (b, c, 0)),
            pl.BlockSpec((1, LANES), lambda b, c, g: (0, 0)),
            pl.BlockSpec((1, LANES), lambda b, c, g: (0, 0)),
            pl.BlockSpec((1, gps * gw), lambda b, c, g: (0, g)),
            pl.BlockSpec((1, L, gps * gw), lambda b, c, g: (b, c, g)),
            pl.BlockSpec((1, gps * gw), lambda b, c, g: (0, g)),
            pl.BlockSpec((L, L), lambda b, c, g: (0, 0)),
            pl.BlockSpec((L, L), lambda b, c, g: (0, 0)),
        ],
        out_specs=pl.BlockSpec((1, L, gps * gw), lambda b, c, g: (b, c, g)),
        out_shape=jax.ShapeDtypeStruct((batch, seq, ssm_w), BF16),
        scratch_shapes=[
            pltpu.VMEM((G, N, gw), F32),
            pltpu.VMEM((G, L, LANES), F32),
            pltpu.VMEM((G, 8, L), F32),
            pltpu.VMEM((LANES, L), F32),
        ],
        compiler_params=_cparams("parallel", "arbitrary", "arbitrary"),
        name="ssd",
    )(xbc3, xbc3, xbc3, dt3, dtb, alog, dskip, z3, nw, tri, ltri)


def _proj_out_kernel(attn_ref, y_ref, an_ref, w_ref, x_ref, o_ref, cat_ref, *, n_heads):
    j = pl.program_id(1)
    aw = n_heads * ATTN_HEAD_DIM

    @pl.when(j == 0)
    def _():
        a = jnp.concatenate([attn_ref[0, hh].astype(F32) for hh in range(n_heads)], axis=1)
        cat_ref[:, 0:aw] = _rms(a, an_ref[...]).astype(BF16)
        cat_ref[:, aw:] = y_ref[...]

    tm = x_ref.shape[0]
    rc = min(tm, ROW_CHUNK)
    starts = range(0, tm, rc)
    accs = [jnp.dot(cat_ref[r0:r0 + rc, :], w_ref[...], preferred_element_type=F32) for r0 in starts]
    for r0, acc in zip(starts, accs):
        o_ref[r0:r0 + rc, :] = x_ref[r0:r0 + rc, :] + acc


def _proj_out(attn, y2, an, w_out, x2, *, seq):
    batch, n_heads, _, dh = attn.shape
    T, D = x2.shape
    aw = n_heads * dh
    sw = y2.shape[1]
    tm = _row_tile(seq, 1024)
    tn = 512
    nsb = seq // tm
    return pl.pallas_call(
        functools.partial(_proj_out_kernel, n_heads=n_heads),
        grid=(T // tm, D // tn),
        in_specs=[
            pl.BlockSpec((1, n_heads, tm, dh), lambda i, j: (i // nsb, 0, i % nsb, 0)),
            pl.BlockSpec((tm, sw), lambda i, j: (i, 0)),
            pl.BlockSpec((1, aw), lambda i, j: (0, 0)),
            pl.BlockSpec((aw + sw, tn), lambda i, j: (0, j)),
            pl.BlockSpec((tm, tn), lambda i, j: (i, j)),
        ],
        out_specs=pl.BlockSpec((tm, tn), lambda i, j: (i, j)),
        out_shape=jax.ShapeDtypeStruct((T, D), F32),
        scratch_shapes=[pltpu.VMEM((tm, aw + sw), BF16)],
        compiler_params=_cparams("parallel", "arbitrary"),
        name="proj_out",
    )(attn, y2, an, w_out, x2)


def _ffn_kernel(x_ref, ln_ref, wg_ref, wv_ref, cwg_ref, cwv_ref, cbg_ref, cbv_ref, wd_ref, fn_ref, o_ref,
                h_ref, halo_ref, ug_ref, uv_ref, *, nsb, final):
    i = pl.program_id(0)
    j = pl.program_id(1)
    tm = x_ref.shape[0]

    @pl.when((i == 0) & (j == 0))
    def _():
        halo_ref[...] = jnp.zeros_like(halo_ref)

    @pl.when(j == 0)
    def _():
        x = x_ref[...]
        h_ref[...] = _rms(x, ln_ref[...]).astype(BF16)
        o_ref[...] = x

    first = i % nsb == 0
    rc = min(tm, FFN_ROW_CHUNK)
    starts = range(0, tm, rc)
    ups = [(jnp.dot(h_ref[r0:r0 + rc, :], wg_ref[...], preferred_element_type=F32),
            jnp.dot(h_ref[r0:r0 + rc, :], wv_ref[...], preferred_element_type=F32)) for r0 in starts]
    ug_ref[0:8, :] = jnp.where(first, 0.0, halo_ref[j, 0])
    uv_ref[0:8, :] = jnp.where(first, 0.0, halo_ref[j, 1])
    halo_ref[j, 0] = ups[-1][0][rc - 8:, :]
    halo_ref[j, 1] = ups[-1][1][rc - 8:, :]

    def conv(u_ref, cw_ref, cb_ref, r0):
        out = cb_ref[...]
        for t in range(FFN_CONV):
            o = 8 - (FFN_CONV - 1) + t + r0
            out = out + cw_ref[t:t + 1, :] * u_ref[o:o + rc, :]
        return out

    for r0, (g, v) in zip(starts, ups):
        ug_ref[8 + r0:8 + r0 + rc, :] = g
        uv_ref[8 + r0:8 + r0 + rc, :] = v
        act = (_silu(conv(ug_ref, cwg_ref, cbg_ref, r0)) * conv(uv_ref, cwv_ref, cbv_ref, r0)).astype(BF16)
        o_ref[r0:r0 + rc, :] += jnp.dot(act, wd_ref[...], preferred_element_type=F32)

    if final:
        @pl.when(j == pl.num_programs(1) - 1)
        def _():
            o_ref[...] = _rms(o_ref[...], fn_ref[...])


def _ffn(x2, ln, w_up, conv_w, conv_b, w_down, fn, *, seq, final):
    T, D = x2.shape
    dff = w_down.shape[0]
    tm = _row_tile(seq, 512)
    tf = 512
    assert dff % tf == 0 and w_up.shape == (D, 2 * dff)
    nsb = seq // tm
    nf = dff // tf
    return pl.pallas_call(
        functools.partial(_ffn_kernel, nsb=nsb, final=final),
        grid=(T // tm, nf),
        in_specs=[
            pl.BlockSpec((tm, D), lambda i, j: (i, 0)),
            pl.BlockSpec((1, D), lambda i, j: (0, 0)),
            pl.BlockSpec((D, tf), lambda i, j: (0, j)),
            pl.BlockSpec((D, tf), lambda i, j: (0, nf + j)),
            pl.BlockSpec((FFN_CONV, tf), lambda i, j: (0, j)),
            pl.BlockSpec((FFN_CONV, tf), lambda i, j: (0, nf + j)),
            pl.BlockSpec((1, tf), lambda i, j: (0, j)),
            pl.BlockSpec((1, tf), lambda i, j: (0, nf + j)),
            pl.BlockSpec((tf, D), lambda i, j: (j, 0)),
            pl.BlockSpec((1, D), lambda i, j: (0, 0)),
        ],
        out_specs=pl.BlockSpec((tm, D), lambda i, j: (i, 0)),
        out_shape=jax.ShapeDtypeStruct((T, D), F32),
        scratch_shapes=[
            pltpu.VMEM((tm, D), BF16),
            pltpu.VMEM((nf, 2, 8, tf), F32),
            pltpu.VMEM((tm + 8, tf), F32),
            pltpu.VMEM((tm + 8, tf), F32),
        ],
        compiler_params=_cparams("arbitrary", "arbitrary"),
        name="ffn_final" if final else "ffn",
    )(x2, ln, w_up, w_up, conv_w, conv_w, conv_b, conv_b, w_down, fn)


def _rope_tables(seq):
    half = ATTN_HEAD_DIM // 2
    inv_freq = jnp.power(ROPE_THETA, -jnp.arange(half, dtype=F32) / half)
    ang = jnp.arange(seq, dtype=F32)[:, None] * inv_freq[None, :]
    cos, sin = jnp.cos(ang), jnp.sin(ang)
    return jnp.concatenate([cos, cos], axis=-1), jnp.concatenate([-sin, sin], axis=-1)


def _chunk_constants():
    L = SSD_CHUNK
    low = np.tril(np.ones((L, L), np.float32))
    tri = jnp.asarray(np.where(low > 0, 0.0, NEG_INF).astype(np.float32))
    ltri = jnp.asarray(low).astype(BF16)
    return tri, ltri


def _block_onehot(seq):
    oh = np.zeros((seq, LANES), np.float32)
    oh[np.arange(seq), np.arange(seq) // MOBA_BLOCK] = 1.0
    return jnp.asarray(oh).astype(BF16)


def _pad_lanes(v):
    return jnp.pad(v, (0, LANES - v.shape[0]))[None, :]


def kernel(x, ln1, w_in, attn_norm, ssm_conv_w, ssm_conv_b, dt_bias, a_log, d_skip, ssm_norm, w_out, ln2, w_up, ffn_conv_w, ffn_conv_b, w_down, final_norm):
    batch, seq, d_model = x.shape
    depth = ln1.shape[0]
    attn_w = attn_norm.shape[1]
    ssm_w = ssm_norm.shape[1]
    xbc_w = ssm_conv_w.shape[2]
    n_ssm_heads = a_log.shape[1]
    assert ssm_w // n_ssm_heads == SSM_HEAD_DIM and n_ssm_heads <= LANES
    main_w = 3 * attn_w + ssm_w + xbc_w
    cos_t, sin_t = _rope_tables(seq)
    tri, ltri = _chunk_constants()
    onehot = _block_onehot(seq)

    x2 = x.reshape(batch * seq, d_model)
    for i in range(depth):
        w_main = w_in[i].astype(BF16)
        w_dt = jnp.pad(w_main[:, main_w:], ((0, 0), (0, LANES - n_ssm_heads)))
        q, k, v, z2, xbc2, dt2 = _proj_in(
            x2, ln1[i][None, :], w_main, w_dt, cos_t, sin_t, ssm_conv_w[i], ssm_conv_b[i][None, :],
            batch=batch, seq=seq, attn_w=attn_w, ssm_w=ssm_w, xbc_w=xbc_w)

        attn = _moba(q, k, v, onehot)

        y3 = _ssd(
            xbc2.reshape(batch, seq, xbc_w), dt2.reshape(batch, seq, LANES),
            z2.reshape(batch, seq, ssm_w),
            _pad_lanes(dt_bias[i]), _pad_lanes(a_log[i]),
            jnp.repeat(d_skip[i], SSM_HEAD_DIM)[None, :], ssm_norm[i][None, :], tri, ltri,
            ssm_w=ssm_w)

        x2 = _proj_out(attn, y3.reshape(batch * seq, ssm_w), attn_norm[i][None, :],
                       w_out[i].astype(BF16), x2, seq=seq)

        x2 = _ffn(x2, ln2[i][None, :], w_up[i].astype(BF16), ffn_conv_w[i], ffn_conv_b[i][None, :],
                  w_down[i].astype(BF16), final_norm[None, :], seq=seq, final=(i == depth - 1))
    return x2.reshape(batch, seq, d_model)
```

```python
import functools

import numpy as np
import jax
import jax.numpy as jnp
from jax import lax
from jax.experimental import pallas as pl
from jax.experimental.pallas import tpu as pltpu

F32 = jnp.float32
BF16 = jnp.bfloat16

NORM_EPS = 1e-6
NEG_INF = -1e30
LOG2E = 1.4426950408889634
ROPE_THETA = 10000.0

ATTN_HEAD_DIM = 128
MOBA_BLOCK = 256
MOBA_TOPK = 3
MOBA_MAX_BLOCKS = 8
MOBA_HEADS_PER_STEP = 2

SSM_HEAD_DIM = 64
SSM_GROUPS = 8
SSM_STATE = 128
SSM_CONV = 4
SSD_CHUNK = 256
SSD_GROUPS_PER_STEP = 8
FFN_CONV = 3
ROW_CHUNK = 256
FFN_ROW_CHUNK = 256

LANES = 128
BF16_ROWS = 16
VMEM_LIMIT_BYTES = 56 * 1024 * 1024


def _cparams(*sem):
    return pltpu.CompilerParams(dimension_semantics=sem, vmem_limit_bytes=VMEM_LIMIT_BYTES)


def _row_tile(seq, pref):
    t = min(seq, pref)
    assert seq % t == 0
    return t


def _rms(xf, g):
    ms = jnp.mean(xf * xf, axis=-1, keepdims=True)
    return xf * lax.rsqrt(ms + NORM_EPS) * g


def _silu(x):
    return x * (1.0 / (1.0 + jnp.exp(-x)))


def _nt_dot(a, b):
    return lax.dot_general(a, b, (((1,), (1,)), ((), ())), preferred_element_type=F32)


def _proj_in_kernel(x_ref, ln_ref, w_ref, wdt_ref, cos_ref, sin_ref, cw_ref, cb_ref,
                    q_ref, k_ref, v_ref, z_ref, xbc_ref, dt_ref,
                    h_ref, halo_ref, *, nq, nz, hpt, nsb):
    i = pl.program_id(0)
    j = pl.program_id(1)
    tm = x_ref.shape[0]

    @pl.when(j == 0)
    def _():
        h = _rms(x_ref[...], ln_ref[...]).astype(BF16)
        h_ref[...] = h
        dt_ref[...] = jnp.dot(h, wdt_ref[...], preferred_element_type=F32)

    rc = min(tm, ROW_CHUNK)
    starts = range(0, tm, rc)

    def mm_chunks():
        return [jnp.dot(h_ref[r0:r0 + rc, :], w_ref[...], preferred_element_type=F32) for r0 in starts]

    def rope_store(o_ref):
        accs = mm_chunks()
        for r0, acc in zip(starts, accs):
            cos = cos_ref[r0:r0 + rc, :]
            sin = sin_ref[r0:r0 + rc, :]
            for hh in range(hpt):
                a = acc[:, hh * LANES:(hh + 1) * LANES]
                o_ref[0, hh, r0:r0 + rc, :] = (
                    a * cos + pltpu.roll(a, ATTN_HEAD_DIM // 2, axis=1) * sin).astype(BF16)

    @pl.when(j < nq)
    def _():
        rope_store(q_ref)

    @pl.when((j >= nq) & (j < 2 * nq))
    def _():
        rope_store(k_ref)

    @pl.when((j >= 2 * nq) & (j < 3 * nq))
    def _():
        accs = mm_chunks()
        for r0, acc in zip(starts, accs):
            for hh in range(hpt):
                v_ref[0, hh, r0:r0 + rc, :] = acc[:, hh * LANES:(hh + 1) * LANES].astype(BF16)

    @pl.when((j >= 3 * nq) & (j < 3 * nq + nz))
    def _():
        accs = mm_chunks()
        for r0, acc in zip(starts, accs):
            z_ref[r0:r0 + rc, :] = _silu(acc).astype(BF16)

    @pl.when((i == 0) & (j == 0))
    def _():
        halo_ref[...] = jnp.zeros_like(halo_ref)

    @pl.when(j >= 3 * nq + nz)
    def _():
        jx = j - (3 * nq + nz)
        prev = jnp.where(i % nsb == 0, 0.0, halo_ref[jx])
        cw = cw_ref[...]
        row = lax.broadcasted_iota(jnp.int32, prev.shape, 0)
        accs = mm_chunks()
        for r0, acc in zip(starts, accs):
            conv = cb_ref[...] + cw[SSM_CONV - 1:SSM_CONV, :] * acc
            for sh in range(1, SSM_CONV):
                r = pltpu.roll(acc, sh, axis=0)
                head = jnp.where(row < sh, pltpu.roll(prev, sh, axis=0), r[0:8])
                xk = jnp.concatenate([head, r[8:]], axis=0)
                conv = conv + cw[SSM_CONV - 1 - sh:SSM_CONV - sh, :] * xk
            xbc_ref[r0:r0 + rc, :] = _silu(conv).astype(BF16)
            prev = acc[rc - 8:, :]
        halo_ref[jx] = prev


def _proj_in(x2, ln, w_main, w_dt, cos_t, sin_t, conv_w, conv_b, *, batch, seq, attn_w, ssm_w, xbc_w):
    T, D = x2.shape
    tm = _row_tile(seq, 1024)
    tn = 512
    nsb = seq // tm
    n_heads = attn_w // ATTN_HEAD_DIM
    hpt = tn // ATTN_HEAD_DIM
    nq = attn_w // tn
    nz = ssm_w // tn
    nx = xbc_w // tn
    nj = 3 * nq + nz + nx
    assert w_main.shape[0] == D and w_main.shape[1] >= nj * tn

    def clampj(lo, n):
        return lambda j: jnp.clip(j - lo, 0, n - 1)

    qj, kj, vj = clampj(0, nq), clampj(nq, nq), clampj(2 * nq, nq)
    zj, xj = clampj(3 * nq, nz), clampj(3 * nq + nz, nx)

    head_shape = jax.ShapeDtypeStruct((batch, n_heads, seq, ATTN_HEAD_DIM), BF16)

    def head_spec(fj):
        return pl.BlockSpec((1, hpt, tm, ATTN_HEAD_DIM), lambda i, j: (i // nsb, fj(j), i % nsb, 0))

    return pl.pallas_call(
        functools.partial(_proj_in_kernel, nq=nq, nz=nz, hpt=hpt, nsb=nsb),
        grid=(T // tm, nj),
        in_specs=[
            pl.BlockSpec((tm, D), lambda i, j: (i, 0)),
            pl.BlockSpec((1, D), lambda i, j: (0, 0)),
            pl.BlockSpec((D, tn), lambda i, j: (0, j)),
            pl.BlockSpec((D, LANES), lambda i, j: (0, 0)),
            pl.BlockSpec((tm, ATTN_HEAD_DIM), lambda i, j: (i % nsb, 0)),
            pl.BlockSpec((tm, ATTN_HEAD_DIM), lambda i, j: (i % nsb, 0)),
            pl.BlockSpec((SSM_CONV, tn), lambda i, j: (0, xj(j))),
            pl.BlockSpec((1, tn), lambda i, j: (0, xj(j))),
        ],
        out_specs=[
            head_spec(qj), head_spec(kj), head_spec(vj),
            pl.BlockSpec((tm, tn), lambda i, j: (i, zj(j))),
            pl.BlockSpec((tm, tn), lambda i, j: (i, xj(j))),
            pl.BlockSpec((tm, LANES), lambda i, j: (i, 0)),
        ],
        out_shape=[
            head_shape, head_shape, head_shape,
            jax.ShapeDtypeStruct((T, ssm_w), BF16),
            jax.ShapeDtypeStruct((T, xbc_w), BF16),
            jax.ShapeDtypeStruct((T, LANES), F32),
        ],
        scratch_shapes=[
            pltpu.VMEM((tm, D), BF16),
            pltpu.VMEM((nx, 8, tn), F32),
        ],
        compiler_params=_cparams("arbitrary", "arbitrary"),
        name="proj_in",
    )(x2, ln, w_main, w_dt, cos_t, sin_t, conv_w, conv_b)


def _moba_kernel(q_ref, k_ref, v_ref, oh_ref, o_ref, qa_ref, *, nb, hps, scale):
    seq = nb * MOBA_BLOCK
    nbp = MOBA_MAX_BLOCKS
    blk = MOBA_BLOCK
    H = range(hps)

    own = jnp.right_shift(lax.broadcasted_iota(jnp.int32, (nbp, seq), 1), blk.bit_length() - 1)
    kb = lax.broadcasted_iota(jnp.int32, (nbp, seq), 0)
    for u in H:
        kf = k_ref[0, u].astype(F32)
        rows = [jnp.sum(kf[b * blk:(b + 1) * blk], axis=0, keepdims=True) for b in range(nb)]
        if nb < nbp:
            rows.append(jnp.zeros((nbp - nb, ATTN_HEAD_DIM), F32))
        kmean = jnp.concatenate(rows, axis=0) * (1.0 / blk)
        k_hi = kmean.astype(BF16)
        k_lo = (kmean - k_hi.astype(F32)).astype(BF16)
        q = q_ref[0, u]
        g2 = _nt_dot(jnp.concatenate([k_hi, k_lo], axis=0), q)
        gate = g2[0:nbp] + g2[nbp:2 * nbp]
        rank = jnp.zeros((nbp, seq), F32)
        for b in range(nb):
            gb = gate[b:b + 1, :]
            beats = (b < own) & ((gb > gate) | ((gb == gate) & (b < kb)))
            rank = rank + jnp.where(beats, 1.0, 0.0)
        allowed = ((kb < own) & (rank < MOBA_TOPK)) | (kb == own)
        bias_t = jnp.where(allowed, 0.0, NEG_INF)
        bias_t = jnp.concatenate([bias_t, jnp.zeros((LANES - nbp, seq), F32)], axis=0)
        qa_ref[u, :, 0:ATTN_HEAD_DIM] = q
        qa_ref[u, :, ATTN_HEAD_DIM:] = bias_t.T.astype(BF16)

    qi = lax.broadcasted_iota(jnp.int32, (blk, blk), 0)
    ki = lax.broadcasted_iota(jnp.int32, (blk, blk), 1)
    causal = ki <= qi
    c2 = scale * LOG2E

    def scores(u, i):
        hi = (i + 1) * blk
        ka = jnp.concatenate([k_ref[0, u, 0:hi, :], oh_ref[0:hi, :]], axis=1)
        return _nt_dot(qa_ref[u, i * blk:hi, :], ka)

    s_next = [scores(u, 0) for u in H]
    for i in range(nb):
        hi = (i + 1) * blk
        s_cur = s_next
        if i + 1 < nb:
            s_next = [scores(u, i + 1) for u in H]
        for u in H:
            s = s_cur[u]
            s_own = jnp.where(causal, s[:, i * blk:], NEG_INF)
            s = jnp.concatenate([s[:, 0:i * blk], s_own], axis=1) if i else s_own
            m = jnp.max(s, axis=1, keepdims=True)
            p = jnp.exp2((s - m) * c2)
            l = jnp.sum(p, axis=1, keepdims=True)
            acc = jnp.dot(p.astype(BF16), v_ref[0, u, 0:hi, :], preferred_element_type=F32)
            o_ref[0, u, i * blk:hi, :] = (acc / l).astype(BF16)


def _moba(q, k, v, onehot):
    batch, n_heads, seq, dh = q.shape
    nb = seq // MOBA_BLOCK
    hps = MOBA_HEADS_PER_STEP
    assert seq % MOBA_BLOCK == 0 and nb <= MOBA_MAX_BLOCKS and dh == ATTN_HEAD_DIM and n_heads % hps == 0
    full = pl.BlockSpec((1, hps, seq, dh), lambda b, h: (b, h, 0, 0))
    return pl.pallas_call(
        functools.partial(_moba_kernel, nb=nb, hps=hps, scale=dh ** -0.5),
        grid=(batch, n_heads // hps),
        in_specs=[full, full, full, pl.BlockSpec((seq, LANES), lambda b, h: (0, 0))],
        out_specs=full,
        out_shape=jax.ShapeDtypeStruct((batch, n_heads, seq, dh), BF16),
        scratch_shapes=[pltpu.VMEM((hps, seq, 2 * dh), BF16)],
        compiler_params=_cparams("parallel", "parallel"),
        name="moba",
    )(q, k, v, onehot)


def _ssd_kernel(xx_ref, xb_ref, xc_ref, dt_ref, dtb_ref, alog_ref, dskip_ref, z_ref, nw_ref,
                tri_ref, ltri_ref, y_ref,
                st_ref, acsc_ref, rows_ref, rowt_ref, *, hpg, gps):
    c = pl.program_id(1)
    gp = pl.program_id(2)
    L = SSD_CHUNK
    P = SSM_HEAD_DIM
    N = SSM_STATE
    gw = hpg * P
    n_groups = st_ref.shape[0]

    @pl.when(gp == 0)
    def _():
        dtv = dt_ref[0] + dtb_ref[...]
        dtv = jnp.maximum(dtv, 0.0) + jnp.log1p(jnp.exp(-jnp.abs(dtv)))
        a = dtv * (-jnp.exp(alog_ref[...]))
        a_hi = a.astype(BF16)
        r1 = a - a_hi.astype(F32)
        a_mid = r1.astype(BF16)
        a_lo = (r1 - a_mid.astype(F32)).astype(BF16)
        cs3 = jnp.dot(ltri_ref[...], jnp.concatenate([a_hi, a_mid, a_lo], axis=1),
                      preferred_element_type=F32)
        acs2 = (cs3[:, 0:LANES] + cs3[:, LANES:2 * LANES] + cs3[:, 2 * LANES:]) * LOG2E
        rowt_ref[...] = (acs2 - jnp.log2(dtv)).T
        for gg in range(n_groups):
            sh = (LANES - hpg * gg) % LANES
            acsc_ref[gg] = pltpu.roll(acs2, sh, axis=1) if sh else acs2
            rows_ref[gg, 0:hpg, :] = rowt_ref[hpg * gg:hpg * (gg + 1), :]

    @pl.when(c == 0)
    def _():
        for u in range(gps):
            st_ref[gp * gps + u] = jnp.zeros((N, gw), F32)

    tri = tri_ref[...]
    lane_head = jnp.right_shift(lax.broadcasted_iota(jnp.int32, (1, gw), 1), P.bit_length() - 1)

    U = range(gps)
    gs = [gp * gps + u for u in U]
    xs_b = [xx_ref[0, :, u * gw:(u + 1) * gw] for u in U]
    bm_b = [xb_ref[0, :, u * N:(u + 1) * N] for u in U]
    cm_b = [xc_ref[0, :, u * N:(u + 1) * N] for u in U]
    cb_mat = [_nt_dot(cm_b[u], bm_b[u]) for u in U]
    bm_t = [bm_b[u].astype(F32).T for u in U]
    acsc = [acsc_ref[gs[u]] for u in U]
    e_col = [jnp.exp2(acsc[u]) for u in U]
    a_end = [acsc[u][L - 1:L, :] for u in U]

    mp = [[] for _ in U]
    bw = [[] for _ in U]
    xm = [[] for _ in U]
    dfs = [jnp.zeros((L, gw), F32) for _ in U]
    dch = [jnp.zeros((1, gw), F32) for _ in U]
    for r in range(hpg):
        hmask = lane_head == r
        for u in U:
            a_col = acsc[u][:, r:r + 1]
            a_row = rows_ref[gs[u], r:r + 1, :]
            a_last = a_end[u][:, r:r + 1]
            mp[u].append((cb_mat[u] * jnp.exp2(a_col - a_row + tri)).astype(BF16))
            bw[u].append((bm_t[u] * jnp.exp2(a_last - a_row)).astype(BF16))
            xm[u].append(jnp.where(hmask, xs_b[u], jnp.zeros_like(xs_b[u])))
            dfs[u] = jnp.where(hmask, jnp.broadcast_to(e_col[u][:, r:r + 1], (L, gw)), dfs[u])
            dch[u] = jnp.where(hmask, jnp.broadcast_to(jnp.exp2(a_last), (1, gw)), dch[u])
    mp = [jnp.concatenate(mp[u], axis=1) for u in U]
    bw = [jnp.concatenate(bw[u], axis=1) for u in U]
    xm = [jnp.concatenate(xm[u], axis=0) for u in U]

    st_old = [st_ref[gs[u]] for u in U]
    y = [jnp.dot(mp[u], xm[u], preferred_element_type=F32) for u in U]
    yo = [jnp.dot(cm_b[u], st_old[u].astype(BF16), preferred_element_type=F32) for u in U]
    sn = [jnp.dot(bw[u], xm[u], preferred_element_type=F32) for u in U]
    for u in U:
        st_ref[gs[u]] = st_old[u] * dch[u] + sn[u]
    for u in U:
        cols = slice(u * gw, (u + 1) * gw)
        yy = y[u] + yo[u] * dfs[u] + dskip_ref[:, cols] * xs_b[u].astype(F32)
        hg = yy * z_ref[0, :, cols].astype(F32)
        y_ref[0, :, cols] = (_rms(hg, nw_ref[:, cols])).astype(BF16)


def _ssd(xbc3, dt3, z3, dtb, alog, dskip, nw, tri, ltri, *, ssm_w):
    batch, seq, xbc_w = xbc3.shape
    L = SSD_CHUNK
    N = SSM_STATE
    G = SSM_GROUPS
    gps = SSD_GROUPS_PER_STEP
    assert seq % L == 0 and G % gps == 0
    nc = seq // L
    gw = ssm_w // G
    hpg = gw // SSM_HEAD_DIM
    assert hpg <= 8 and xbc_w == ssm_w + 2 * G * N and gw % LANES == 0
    b_off = ssm_w // (gps * N)
    c_off = b_off + G // gps

    return pl.pallas_call(
        functools.partial(_ssd_kernel, hpg=hpg, gps=gps),
        grid=(batch, nc, G // gps),
        in_specs=[
            pl.BlockSpec((1, L, gps * gw), lambda b, c, g: (b, c, g)),
            pl.BlockSpec((1, L, gps * N), lambda b, c, g: (b, c, b_off + g)),
            pl.BlockSpec((1, L, gps * N), lambda b, c, g: (b, c, c_off + g)),
            pl.BlockSpec((1, L, LANES), lambda b, c, g: (b, c, 0)),
            pl.BlockSpec((1, LANES), lambda b, c, g: (0, 0)),
            pl.BlockSpec((1, LANES), lambda b, c, g: (0, 0)),
            pl.BlockSpec((1, gps * gw), lambda b, c, g: (0, g)),
            pl.BlockSpec((1, L, gps * gw), lambda b, c, g: (b, c, g)),
            pl.BlockSpec((1, gps * gw), lambda b, c, g: (0, g)),
            pl.BlockSpec((L, L), lambda b, c, g: (0, 0)),
            pl.BlockSpec((L, L), lambda b, c, g: (0, 0)),
        ],
        out_specs=pl.BlockSpec((1, L, gps * gw), lambda b, c, g: (b, c, g)),
        out_shape=jax.ShapeDtypeStruct((batch, seq, ssm_w), BF16),
        scratch_shapes=[
            pltpu.VMEM((G, N, gw), F32),
            pltpu.VMEM((G, L, LANES), F32),
            pltpu.VMEM((G, 8, L), F32),
            pltpu.VMEM((LANES, L), F32),
        ],
        compiler_params=_cparams("parallel", "arbitrary", "arbitrary"),
        name="ssd",
    )(xbc3, xbc3, xbc3, dt3, dtb, alog, dskip, z3, nw, tri, ltri)


def _proj_out_kernel(attn_ref, y_ref, an_ref, w_ref, x_ref, o_ref, cat_ref, *, n_heads):
    j = pl.program_id(1)
    aw = n_heads * ATTN_HEAD_DIM

    @pl.when(j == 0)
    def _():
        a = jnp.concatenate([attn_ref[0, hh].astype(F32) for hh in range(n_heads)], axis=1)
        cat_ref[:, 0:aw] = _rms(a, an_ref[...]).astype(BF16)
        cat_ref[:, aw:] = y_ref[...]

    tm = x_ref.shape[0]
    rc = min(tm, ROW_CHUNK)
    starts = range(0, tm, rc)
    accs = [jnp.dot(cat_ref[r0:r0 + rc, :], w_ref[...], preferred_element_type=F32) for r0 in starts]
    for r0, acc in zip(starts, accs):
        o_ref[r0:r0 + rc, :] = x_ref[r0:r0 + rc, :] + acc


def _proj_out(attn, y2, an, w_out, x2, *, seq):
    batch, n_heads, _, dh = attn.shape
    T, D = x2.shape
    aw = n_heads * dh
    sw = y2.shape[1]
    tm = _row_tile(seq, 1024)
    tn = 512
    nsb = seq // tm
    return pl.pallas_call(
        functools.partial(_proj_out_kernel, n_heads=n_heads),
        grid=(T // tm, D // tn),
        in_specs=[
            pl.BlockSpec((1, n_heads, tm, dh), lambda i, j: (i // nsb, 0, i % nsb, 0)),
            pl.BlockSpec((tm, sw), lambda i, j: (i, 0)),
            pl.BlockSpec((1, aw), lambda i, j: (0, 0)),
            pl.BlockSpec((aw + sw, tn), lambda i, j: (0, j)),
            pl.BlockSpec((tm, tn), lambda i, j: (i, j)),
        ],
        out_specs=pl.BlockSpec((tm, tn), lambda i, j: (i, j)),
        out_shape=jax.ShapeDtypeStruct((T, D), F32),
        scratch_shapes=[pltpu.VMEM((tm, aw + sw), BF16)],
        compiler_params=_cparams("parallel", "arbitrary"),
        name="proj_out",
    )(attn, y2, an, w_out, x2)


def _ffn_kernel(x_ref, ln_ref, wg_ref, wv_ref, cwg_ref, cwv_ref, cbg_ref, cbv_ref, wd_ref, fn_ref, o_ref,
                h_ref, halo_ref, ug_ref, uv_ref, *, nsb, final):
    i = pl.program_id(0)
    j = pl.program_id(1)
    tm = x_ref.shape[0]

    @pl.when((i == 0) & (j == 0))
    def _():
        halo_ref[...] = jnp.zeros_like(halo_ref)

    @pl.when(j == 0)
    def _():
        x = x_ref[...]
        h_ref[...] = _rms(x, ln_ref[...]).astype(BF16)
        o_ref[...] = x

    first = i % nsb == 0
    rc = min(tm, FFN_ROW_CHUNK)
    starts = range(0, tm, rc)
    ups = [(jnp.dot(h_ref[r0:r0 + rc, :], wg_ref[...], preferred_element_type=F32),
            jnp.dot(h_ref[r0:r0 + rc, :], wv_ref[...], preferred_element_type=F32)) for r0 in starts]
    ug_ref[0:8, :] = jnp.where(first, 0.0, halo_ref[j, 0])
    uv_ref[0:8, :] = jnp.where(first, 0.0, halo_ref[j, 1])
    halo_ref[j, 0] = ups[-1][0][rc - 8:, :]
    halo_ref[j, 1] = ups[-1][1][rc - 8:, :]

    def conv(u_ref, cw_ref, cb_ref, r0):
        out = cb_ref[...]
        for t in range(FFN_CONV):
            o = 8 - (FFN_CONV - 1) + t + r0
            out = out + cw_ref[t:t + 1, :] * u_ref[o:o + rc, :]
        return out

    for r0, (g, v) in zip(starts, ups):
        ug_ref[8 + r0:8 + r0 + rc, :] = g
        uv_ref[8 + r0:8 + r0 + rc, :] = v
        act = (_silu(conv(ug_ref, cwg_ref, cbg_ref, r0)) * conv(uv_ref, cwv_ref, cbv_ref, r0)).astype(BF16)
        o_ref[r0:r0 + rc, :] += jnp.dot(act, wd_ref[...], preferred_element_type=F32)

    if final:
        @pl.when(j == pl.num_programs(1) - 1)
        def _():
            o_ref[...] = _rms(o_ref[...], fn_ref[...])


def _ffn(x2, ln, w_up, conv_w, conv_b, w_down, fn, *, seq, final):
    T, D = x2.shape
    dff = w_down.shape[0]
    tm = _row_tile(seq, 512)
    tf = 512
    assert dff % tf == 0 and w_up.shape == (D, 2 * dff)
    nsb = seq // tm
    nf = dff // tf
    return pl.pallas_call(
        functools.partial(_ffn_kernel, nsb=nsb, final=final),
        grid=(T // tm, nf),
        in_specs=[
            pl.BlockSpec((tm, D), lambda i, j: (i, 0)),
            pl.BlockSpec((1, D), lambda i, j: (0, 0)),
            pl.BlockSpec((D, tf), lambda i, j: (0, j)),
            pl.BlockSpec((D, tf), lambda i, j: (0, nf + j)),
            pl.BlockSpec((FFN_CONV, tf), lambda i, j: (0, j)),
            pl.BlockSpec((FFN_CONV, tf), lambda i, j: (0, nf + j)),
            pl.BlockSpec((1, tf), lambda i, j: (0, j)),
            pl.BlockSpec((1, tf), lambda i, j: (0, nf + j)),
            pl.BlockSpec((tf, D), lambda i, j: (j, 0)),
            pl.BlockSpec((1, D), lambda i, j: (0, 0)),
        ],
        out_specs=pl.BlockSpec((tm, D), lambda i, j: (i, 0)),
        out_shape=jax.ShapeDtypeStruct((T, D), F32),
        scratch_shapes=[
            pltpu.VMEM((tm, D), BF16),
            pltpu.VMEM((nf, 2, 8, tf), F32),
            pltpu.VMEM((tm + 8, tf), F32),
            pltpu.VMEM((tm + 8, tf), F32),
        ],
        compiler_params=_cparams("arbitrary", "arbitrary"),
        name="ffn_final" if final else "ffn",
    )(x2, ln, w_up, w_up, conv_w, conv_w, conv_b, conv_b, w_down, fn)


def _rope_tables(seq):
    half = ATTN_HEAD_DIM // 2
    inv_freq = jnp.power(ROPE_THETA, -jnp.arange(half, dtype=F32) / half)
    ang = jnp.arange(seq, dtype=F32)[:, None] * inv_freq[None, :]
    cos, sin = jnp.cos(ang), jnp.sin(ang)
    return jnp.concatenate([cos, cos], axis=-1), jnp.concatenate([-sin, sin], axis=-1)


def _chunk_constants():
    L = SSD_CHUNK
    low = np.tril(np.ones((L, L), np.float32))
    tri = jnp.asarray(np.where(low > 0, 0.0, NEG_INF).astype(np.float32))
    ltri = jnp.asarray(low).astype(BF16)
    return tri, ltri


def _block_onehot(seq):
    oh = np.zeros((seq, LANES), np.float32)
    oh[np.arange(seq), np.arange(seq) // MOBA_BLOCK] = 1.0
    return jnp.asarray(oh).astype(BF16)


def _pad_lanes(v):
    return jnp.pad(v, (0, LANES - v.shape[0]))[None, :]


def kernel(x, ln1, w_in, attn_norm, ssm_conv_w, ssm_conv_b, dt_bias, a_log, d_skip, ssm_norm, w_out, ln2, w_up, ffn_conv_w, ffn_conv_b, w_down, final_norm):
    batch, seq, d_model = x.shape
    depth = ln1.shape[0]
    attn_w = attn_norm.shape[1]
    ssm_w = ssm_norm.shape[1]
    xbc_w = ssm_conv_w.shape[2]
    n_ssm_heads = a_log.shape[1]
    assert ssm_w // n_ssm_heads == SSM_HEAD_DIM and n_ssm_heads <= LANES
    main_w = 3 * attn_w + ssm_w + xbc_w
    cos_t, sin_t = _rope_tables(seq)
    tri, ltri = _chunk_constants()
    onehot = _block_onehot(seq)

    x2 = x.reshape(batch * seq, d_model)
    for i in range(depth):
        w_main = w_in[i].astype(BF16)
        w_dt = jnp.pad(w_main[:, main_w:], ((0, 0), (0, LANES - n_ssm_heads)))
        q, k, v, z2, xbc2, dt2 = _proj_in(
            x2, ln1[i][None, :], w_main, w_dt, cos_t, sin_t, ssm_conv_w[i], ssm_conv_b[i][None, :],
            batch=batch, seq=seq, attn_w=attn_w, ssm_w=ssm_w, xbc_w=xbc_w)

        attn = _moba(q, k, v, onehot)

        y3 = _ssd(
            xbc2.reshape(batch, seq, xbc_w), dt2.reshape(batch, seq, LANES),
            z2.reshape(batch, seq, ssm_w),
            _pad_lanes(dt_bias[i]), _pad_lanes(a_log[i]),
            jnp.repeat(d_skip[i], SSM_HEAD_DIM)[None, :], ssm_norm[i][None, :], tri, ltri,
            ssm_w=ssm_w)

        x2 = _proj_out(attn, y3.reshape(batch * seq, ssm_w), attn_norm[i][None, :],
                       w_out[i].astype(BF16), x2, seq=seq)

        x2 = _ffn(x2, ln2[i][None, :], w_up[i].astype(BF16), ffn_conv_w[i], ffn_conv_b[i][None, :],
                  w_down[i].astype(BF16), final_norm[None, :], seq=seq, final=(i == depth - 1))
    return x2.reshape(batch, seq, d_model)
```

```python
import functools

import numpy as np
import jax
import jax.numpy as jnp
from jax import lax
from jax.experimental import pallas as pl
from jax.experimental.pallas import tpu as pltpu

F32 = jnp.float32
BF16 = jnp.bfloat16

NORM_EPS = 1e-6
NEG_INF = -1e30
LOG2E = 1.4426950408889634
ROPE_THETA = 10000.0

ATTN_HEAD_DIM = 128
MOBA_BLOCK = 256
MOBA_TOPK = 3
MOBA_MAX_BLOCKS = 8
MOBA_HEADS_PER_STEP = 2

SSM_HEAD_DIM = 64
SSM_GROUPS = 8
SSM_STATE = 128
SSM_CONV = 4
SSD_CHUNK = 256
SSD_GROUPS_PER_STEP = 8
FFN_CONV = 3
ROW_CHUNK = 512
FFN_ROW_CHUNK = 256

LANES = 128
BF16_ROWS = 16
VMEM_LIMIT_BYTES = 56 * 1024 * 1024


def _cparams(*sem):
    return pltpu.CompilerParams(dimension_semantics=sem, vmem_limit_bytes=VMEM_LIMIT_BYTES)


def _row_tile(seq, pref):
    t = min(seq, pref)
    assert seq % t == 0
    return t


def _rms(xf, g):
    ms = jnp.mean(xf * xf, axis=-1, keepdims=True)
    return xf * lax.rsqrt(ms + NORM_EPS) * g


def _silu(x):
    return x * (1.0 / (1.0 + jnp.exp(-x)))


def _nt_dot(a, b):
    return lax.dot_general(a, b, (((1,), (1,)), ((), ())), preferred_element_type=F32)


def _proj_in_kernel(x_ref, ln_ref, w_ref, wdt_ref, cos_ref, sin_ref, cw_ref, cb_ref,
                    q_ref, k_ref, v_ref, z_ref, xbc_ref, dt_ref,
                    h_ref, halo_ref, *, nq, nz, hpt, nsb):
    i = pl.program_id(0)
    j = pl.program_id(1)
    tm = x_ref.shape[0]

    @pl.when(j == 0)
    def _():
        h = _rms(x_ref[...], ln_ref[...]).astype(BF16)
        h_ref[...] = h
        dt_ref[...] = jnp.dot(h, wdt_ref[...], preferred_element_type=F32)

    rc = min(tm, ROW_CHUNK)
    starts = range(0, tm, rc)

    def mm_chunks():
        return [jnp.dot(h_ref[r0:r0 + rc, :], w_ref[...], preferred_element_type=F32) for r0 in starts]

    def rope_store(o_ref):
        accs = mm_chunks()
        for r0, acc in zip(starts, accs):
            cos = cos_ref[r0:r0 + rc, :]
            sin = sin_ref[r0:r0 + rc, :]
            for hh in range(hpt):
                a = acc[:, hh * LANES:(hh + 1) * LANES]
                o_ref[0, hh, r0:r0 + rc, :] = (
                    a * cos + pltpu.roll(a, ATTN_HEAD_DIM // 2, axis=1) * sin).astype(BF16)

    @pl.when(j < nq)
    def _():
        rope_store(q_ref)

    @pl.when((j >= nq) & (j < 2 * nq))
    def _():
        rope_store(k_ref)

    @pl.when((j >= 2 * nq) & (j < 3 * nq))
    def _():
        accs = mm_chunks()
        for r0, acc in zip(starts, accs):
            for hh in range(hpt):
                v_ref[0, hh, r0:r0 + rc, :] = acc[:, hh * LANES:(hh + 1) * LANES].astype(BF16)

    @pl.when((j >= 3 * nq) & (j < 3 * nq + nz))
    def _():
        accs = mm_chunks()
        for r0, acc in zip(starts, accs):
            z_ref[r0:r0 + rc, :] = _silu(acc).astype(BF16)

    @pl.when((i == 0) & (j == 0))
    def _():
        halo_ref[...] = jnp.zeros_like(halo_ref)

    @pl.when(j >= 3 * nq + nz)
    def _():
        jx = j - (3 * nq + nz)
        prev = jnp.where(i % nsb == 0, 0.0, halo_ref[jx])
        cw = cw_ref[...]
        row = lax.broadcasted_iota(jnp.int32, prev.shape, 0)
        accs = mm_chunks()
        for r0, acc in zip(starts, accs):
            conv = cb_ref[...] + cw[SSM_CONV - 1:SSM_CONV, :] * acc
            for sh in range(1, SSM_CONV):
                r = pltpu.roll(acc, sh, axis=0)
                head = jnp.where(row < sh, pltpu.roll(prev, sh, axis=0), r[0:8])
                xk = jnp.concatenate([head, r[8:]], axis=0)
                conv = conv + cw[SSM_CONV - 1 - sh:SSM_CONV - sh, :] * xk
            xbc_ref[r0:r0 + rc, :] = _silu(conv).astype(BF16)
            prev = acc[rc - 8:, :]
        halo_ref[jx] = prev


def _proj_in(x2, ln, w_main, w_dt, cos_t, sin_t, conv_w, conv_b, *, batch, seq, attn_w, ssm_w, xbc_w):
    T, D = x2.shape
    tm = _row_tile(seq, 1024)
    tn = 512
    nsb = seq // tm
    n_heads = attn_w // ATTN_HEAD_DIM
    hpt = tn // ATTN_HEAD_DIM
    nq = attn_w // tn
    nz = ssm_w // tn
    nx = xbc_w // tn
    nj = 3 * nq + nz + nx
    assert w_main.shape[0] == D and w_main.shape[1] >= nj * tn

    def clampj(lo, n):
        return lambda j: jnp.clip(j - lo, 0, n - 1)

    qj, kj, vj = clampj(0, nq), clampj(nq, nq), clampj(2 * nq, nq)
    zj, xj = clampj(3 * nq, nz), clampj(3 * nq + nz, nx)

    head_shape = jax.ShapeDtypeStruct((batch, n_heads, seq, ATTN_HEAD_DIM), BF16)

    def head_spec(fj):
        return pl.BlockSpec((1, hpt, tm, ATTN_HEAD_DIM), lambda i, j: (i // nsb, fj(j), i % nsb, 0))

    return pl.pallas_call(
        functools.partial(_proj_in_kernel, nq=nq, nz=nz, hpt=hpt, nsb=nsb),
        grid=(T // tm, nj),
        in_specs=[
            pl.BlockSpec((tm, D), lambda i, j: (i, 0)),
            pl.BlockSpec((1, D), lambda i, j: (0, 0)),
            pl.BlockSpec((D, tn), lambda i, j: (0, j)),
            pl.BlockSpec((D, LANES), lambda i, j: (0, 0)),
            pl.BlockSpec((tm, ATTN_HEAD_DIM), lambda i, j: (i % nsb, 0)),
            pl.BlockSpec((tm, ATTN_HEAD_DIM), lambda i, j: (i % nsb, 0)),
            pl.BlockSpec((SSM_CONV, tn), lambda i, j: (0, xj(j))),
            pl.BlockSpec((1, tn), lambda i, j: (0, xj(j))),
        ],
        out_specs=[
            head_spec(qj), head_spec(kj), head_spec(vj),
            pl.BlockSpec((tm, tn), lambda i, j: (i, zj(j))),
            pl.BlockSpec((tm, tn), lambda i, j: (i, xj(j))),
            pl.BlockSpec((tm, LANES), lambda i, j: (i, 0)),
        ],
        out_shape=[
            head_shape, head_shape, head_shape,
            jax.ShapeDtypeStruct((T, ssm_w), BF16),
            jax.ShapeDtypeStruct((T, xbc_w), BF16),
            jax.ShapeDtypeStruct((T, LANES), F32),
        ],
        scratch_shapes=[
            pltpu.VMEM((tm, D), BF16),
            pltpu.VMEM((nx, 8, tn), F32),
        ],
        compiler_params=_cparams("arbitrary", "arbitrary"),
        name="proj_in",
    )(x2, ln, w_main, w_dt, cos_t, sin_t, conv_w, conv_b)


def _moba_kernel(q_ref, k_ref, v_ref, oh_ref, o_ref, qa_ref, *, nb, hps, scale):
    seq = nb * MOBA_BLOCK
    nbp = MOBA_MAX_BLOCKS
    blk = MOBA_BLOCK
    H = range(hps)

    own = jnp.right_shift(lax.broadcasted_iota(jnp.int32, (nbp, seq), 1), blk.bit_length() - 1)
    kb = lax.broadcasted_iota(jnp.int32, (nbp, seq), 0)
    for u in H:
        kf = k_ref[0, u].astype(F32)
        rows = [jnp.sum(kf[b * blk:(b + 1) * blk], axis=0, keepdims=True) for b in range(nb)]
        if nb < nbp:
            rows.append(jnp.zeros((nbp - nb, ATTN_HEAD_DIM), F32))
        kmean = jnp.concatenate(rows, axis=0) * (1.0 / blk)
        k_hi = kmean.astype(BF16)
        k_lo = (kmean - k_hi.astype(F32)).astype(BF16)
        q = q_ref[0, u]
        g2 = _nt_dot(jnp.concatenate([k_hi, k_lo], axis=0), q)
        gate = g2[0:nbp] + g2[nbp:2 * nbp]
        rank = jnp.zeros((nbp, seq), F32)
        for b in range(nb):
            gb = gate[b:b + 1, :]
            beats = (b < own) & ((gb > gate) | ((gb == gate) & (b < kb)))
            rank = rank + jnp.where(beats, 1.0, 0.0)
        allowed = ((kb < own) & (rank < MOBA_TOPK)) | (kb == own)
        bias_t = jnp.where(allowed, 0.0, NEG_INF)
        bias_t = jnp.concatenate([bias_t, jnp.zeros((LANES - nbp, seq), F32)], axis=0)
        qa_ref[u, :, 0:ATTN_HEAD_DIM] = q
        qa_ref[u, :, ATTN_HEAD_DIM:] = bias_t.T.astype(BF16)

    qi = lax.broadcasted_iota(jnp.int32, (blk, blk), 0)
    ki = lax.broadcasted_iota(jnp.int32, (blk, blk), 1)
    causal = ki <= qi
    c2 = scale * LOG2E

    def scores(u, i):
        hi = (i + 1) * blk
        ka = jnp.concatenate([k_ref[0, u, 0:hi, :], oh_ref[0:hi, :]], axis=1)
        return _nt_dot(qa_ref[u, i * blk:hi, :], ka)

    s_next = [scores(u, 0) for u in H]
    for i in range(nb):
        hi = (i + 1) * blk
        s_cur = s_next
        if i + 1 < nb:
            s_next = [scores(u, i + 1) for u in H]
        for u in H:
            s = s_cur[u]
            s_own = jnp.where(causal, s[:, i * blk:], NEG_INF)
            s = jnp.concatenate([s[:, 0:i * blk], s_own], axis=1) if i else s_own
            m = jnp.max(s, axis=1, keepdims=True)
            p = jnp.exp2((s - m) * c2)
            l = jnp.sum(p, axis=1, keepdims=True)
            acc = jnp.dot(p.astype(BF16), v_ref[0, u, 0:hi, :], preferred_element_type=F32)
            o_ref[0, u, i * blk:hi, :] = (acc / l).astype(BF16)


def _moba(q, k, v, onehot):
    batch, n_heads, seq, dh = q.shape
    nb = seq // MOBA_BLOCK
    hps = MOBA_HEADS_PER_STEP
    assert seq % MOBA_BLOCK == 0 and nb <= MOBA_MAX_BLOCKS and dh == ATTN_HEAD_DIM and n_heads % hps == 0
    full = pl.BlockSpec((1, hps, seq, dh), lambda b, h: (b, h, 0, 0))
    return pl.pallas_call(
        functools.partial(_moba_kernel, nb=nb, hps=hps, scale=dh ** -0.5),
        grid=(batch, n_heads // hps),
        in_specs=[full, full, full, pl.BlockSpec((seq, LANES), lambda b, h: (0, 0))],
        out_specs=full,
        out_shape=jax.ShapeDtypeStruct((batch, n_heads, seq, dh), BF16),
        scratch_shapes=[pltpu.VMEM((hps, seq, 2 * dh), BF16)],
        compiler_params=_cparams("parallel", "parallel"),
        name="moba",
    )(q, k, v, onehot)


def _ssd_kernel(xx_ref, xb_ref, xc_ref, dt_ref, dtb_ref, alog_ref, dskip_ref, z_ref, nw_ref,
                tri_ref, ltri_ref, y_ref,
                st_ref, acsc_ref, rows_ref, rowt_ref, *, hpg, gps):
    c = pl.program_id(1)
    gp = pl.program_id(2)
    L = SSD_CHUNK
    P = SSM_HEAD_DIM
    N = SSM_STATE
    gw = hpg * P
    n_groups = st_ref.shape[0]

    @pl.when(gp == 0)
    def _():
        dtv = dt_ref[0] + dtb_ref[...]
        dtv = jnp.maximum(dtv, 0.0) + jnp.log1p(jnp.exp(-jnp.abs(dtv)))
        a = dtv * (-jnp.exp(alog_ref[...]))
        a_hi = a.astype(BF16)
        r1 = a - a_hi.astype(F32)
        a_mid = r1.astype(BF16)
        a_lo = (r1 - a_mid.astype(F32)).astype(BF16)
        cs3 = jnp.dot(ltri_ref[...], jnp.concatenate([a_hi, a_mid, a_lo], axis=1),
                      preferred_element_type=F32)
        acs2 = (cs3[:, 0:LANES] + cs3[:, LANES:2 * LANES] + cs3[:, 2 * LANES:]) * LOG2E
        rowt_ref[...] = (acs2 - jnp.log2(dtv)).T
        for gg in range(n_groups):
            sh = (LANES - hpg * gg) % LANES
            acsc_ref[gg] = pltpu.roll(acs2, sh, axis=1) if sh else acs2
            rows_ref[gg, 0:hpg, :] = rowt_ref[hpg * gg:hpg * (gg + 1), :]

    @pl.when(c == 0)
    def _():
        for u in range(gps):
            st_ref[gp * gps + u] = jnp.zeros((N, gw), F32)

    tri = tri_ref[...]
    lane_head = jnp.right_shift(lax.broadcasted_iota(jnp.int32, (1, gw), 1), P.bit_length() - 1)

    U = range(gps)
    gs = [gp * gps + u for u in U]
    xs_b = [xx_ref[0, :, u * gw:(u + 1) * gw] for u in U]
    bm_b = [xb_ref[0, :, u * N:(u + 1) * N] for u in U]
    cm_b = [xc_ref[0, :, u * N:(u + 1) * N] for u in U]
    cb_mat = [_nt_dot(cm_b[u], bm_b[u]) for u in U]
    bm_t = [bm_b[u].astype(F32).T for u in U]
    acsc = [acsc_ref[gs[u]] for u in U]
    e_col = [jnp.exp2(acsc[u]) for u in U]
    a_end = [acsc[u][L - 1:L, :] for u in U]

    mp = [[] for _ in U]
    bw = [[] for _ in U]
    xm = [[] for _ in U]
    dfs = [jnp.zeros((L, gw), F32) for _ in U]
    dch = [jnp.zeros((1, gw), F32) for _ in U]
    for r in range(hpg):
        hmask = lane_head == r
        for u in U:
            a_col = acsc[u][:, r:r + 1]
            a_row = rows_ref[gs[u], r:r + 1, :]
            a_last = a_end[u][:, r:r + 1]
            mp[u].append((cb_mat[u] * jnp.exp2(a_col - a_row + tri)).astype(BF16))
            bw[u].append((bm_t[u] * jnp.exp2(a_last - a_row)).astype(BF16))
            xm[u].append(jnp.where(hmask, xs_b[u], jnp.zeros_like(xs_b[u])))
            dfs[u] = jnp.where(hmask, jnp.broadcast_to(e_col[u][:, r:r + 1], (L, gw)), dfs[u])
            dch[u] = jnp.where(hmask, jnp.broadcast_to(jnp.exp2(a_last), (1, gw)), dch[u])
    mp = [jnp.concatenate(mp[u], axis=1) for u in U]
    bw = [jnp.concatenate(bw[u], axis=1) for u in U]
    xm = [jnp.concatenate(xm[u], axis=0) for u in U]

    st_old = [st_ref[gs[u]] for u in U]
    y = [jnp.dot(mp[u], xm[u], preferred_element_type=F32) for u in U]
    yo = [jnp.dot(cm_b[u], st_old[u].astype(BF16), preferred_element_type=F32) for u in U]
    sn = [jnp.dot(bw[u], xm[u], preferred_element_type=F32) for u in U]
    for u in U:
        st_ref[gs[u]] = st_old[u] * dch[u] + sn[u]
    for u in U:
        cols = slice(u * gw, (u + 1) * gw)
        yy = y[u] + yo[u] * dfs[u] + dskip_ref[:, cols] * xs_b[u].astype(F32)
        hg = yy * z_ref[0, :, cols].astype(F32)
        y_ref[0, :, cols] = (_rms(hg, nw_ref[:, cols])).astype(BF16)


def _ssd(xbc3, dt3, z3, dtb, alog, dskip, nw, tri, ltri, *, ssm_w):
    batch, seq, xbc_w = xbc3.shape
    L = SSD_CHUNK
    N = SSM_STATE
    G = SSM_GROUPS
    gps = SSD_GROUPS_PER_STEP
    assert seq % L == 0 and G % gps == 0
    nc = seq // L
    gw = ssm_w // G
    hpg = gw // SSM_HEAD_DIM
    assert hpg <= 8 and xbc_w == ssm_w + 2 * G * N and gw % LANES == 0
    b_off = ssm_w // (gps * N)
    c_off = b_off + G // gps

    return pl.pallas_call(
        functools.partial(_ssd_kernel, hpg=hpg, gps=gps),
        grid=(batch, nc, G // gps),
        in_specs=[
            pl.BlockSpec((1, L, gps * gw), lambda b, c, g: (b, c, g)),
            pl.BlockSpec((1, L, gps * N), lambda b, c, g: (b, c, b_off + g)),
            pl.BlockSpec((1, L, gps * N), lambda b, c, g: (b, c, c_off + g)),
            pl.BlockSpec((1, L, LANES), lambda b, c, g: (b, c, 0)),
            pl.BlockSpec((1, LANES), lambda b, c, g: (0, 0)),
            pl.BlockSpec((1, LANES), lambda b, c, g: (0, 0)),
            pl.BlockSpec((1, gps * gw), lambda b, c, g: (0, g)),
            pl.BlockSpec((1, L, gps * gw), lambda b, c, g: (b, c, g)),
            pl.BlockSpec((1, gps * gw), lambda b, c, g: (0, g)),
            pl.BlockSpec((L, L), lambda b, c, g: (0, 0)),
            pl.BlockSpec((L, L), lambda b, c, g: (0, 0)),
        ],
        out_specs=pl.BlockSpec((1, L, gps * gw), lambda b, c, g: (b, c, g)),
        out_shape=jax.ShapeDtypeStruct((batch, seq, ssm_w), BF16),
        scratch_shapes=[
            pltpu.VMEM((G, N, gw), F32),
            pltpu.VMEM((G, L, LANES), F32),
            pltpu.VMEM((G, 8, L), F32),
            pltpu.VMEM((LANES, L), F32),
        ],
        compiler_params=_cparams("parallel", "arbitrary", "arbitrary"),
        name="ssd",
    )(xbc3, xbc3, xbc3, dt3, dtb, alog, dskip, z3, nw, tri, ltri)


def _proj_out_kernel(attn_ref, y_ref, an_ref, w_ref, x_ref, o_ref, cat_ref, *, n_heads):
    j = pl.program_id(1)
    aw = n_heads * ATTN_HEAD_DIM

    @pl.when(j == 0)
    def _():
        a = jnp.concatenate([attn_ref[0, hh].astype(F32) for hh in range(n_heads)], axis=1)
        cat_ref[:, 0:aw] = _rms(a, an_ref[...]).astype(BF16)
        cat_ref[:, aw:] = y_ref[...]

    tm = x_ref.shape[0]
    rc = min(tm, ROW_CHUNK)
    starts = range(0, tm, rc)
    accs = [jnp.dot(cat_ref[r0:r0 + rc, :], w_ref[...], preferred_element_type=F32) for r0 in starts]
    for r0, acc in zip(starts, accs):
        o_ref[r0:r0 + rc, :] = x_ref[r0:r0 + rc, :] + acc


def _proj_out(attn, y2, an, w_out, x2, *, seq):
    batch, n_heads, _, dh = attn.shape
    T, D = x2.shape
    aw = n_heads * dh
    sw = y2.shape[1]
    tm = _row_tile(seq, 1024)
    tn = 512
    nsb = seq // tm
    return pl.pallas_call(
        functools.partial(_proj_out_kernel, n_heads=n_heads),
        grid=(T // tm, D // tn),
        in_specs=[
            pl.BlockSpec((1, n_heads, tm, dh), lambda i, j: (i // nsb, 0, i % nsb, 0)),
            pl.BlockSpec((tm, sw), lambda i, j: (i, 0)),
            pl.BlockSpec((1, aw), lambda i, j: (0, 0)),
            pl.BlockSpec((aw + sw, tn), lambda i, j: (0, j)),
            pl.BlockSpec((tm, tn), lambda i, j: (i, j)),
        ],
        out_specs=pl.BlockSpec((tm, tn), lambda i, j: (i, j)),
        out_shape=jax.ShapeDtypeStruct((T, D), F32),
        scratch_shapes=[pltpu.VMEM((tm, aw + sw), BF16)],
        compiler_params=_cparams("parallel", "arbitrary"),
        name="proj_out",
    )(attn, y2, an, w_out, x2)


def _ffn_kernel(x_ref, ln_ref, wg_ref, wv_ref, cwg_ref, cwv_ref, cbg_ref, cbv_ref, wd_ref, fn_ref, o_ref,
                h_ref, halo_ref, ug_ref, uv_ref, *, nsb, final):
    i = pl.program_id(0)
    j = pl.program_id(1)
    tm = x_ref.shape[0]

    @pl.when((i == 0) & (j == 0))
    def _():
        halo_ref[...] = jnp.zeros_like(halo_ref)

    @pl.when(j == 0)
    def _():
        x = x_ref[...]
        h_ref[...] = _rms(x, ln_ref[...]).astype(BF16)
        o_ref[...] = x

    first = i % nsb == 0
    rc = min(tm, FFN_ROW_CHUNK)
    starts = range(0, tm, rc)
    ups = [(jnp.dot(h_ref[r0:r0 + rc, :], wg_ref[...], preferred_element_type=F32),
            jnp.dot(h_ref[r0:r0 + rc, :], wv_ref[...], preferred_element_type=F32)) for r0 in starts]
    ug_ref[0:8, :] = jnp.where(first, 0.0, halo_ref[j, 0])
    uv_ref[0:8, :] = jnp.where(first, 0.0, halo_ref[j, 1])
    halo_ref[j, 0] = ups[-1][0][rc - 8:, :]
    halo_ref[j, 1] = ups[-1][1][rc - 8:, :]

    def conv(u_ref, cw_ref, cb_ref, r0):
        out = cb_ref[...]
        for t in range(FFN_CONV):
            o = 8 - (FFN_CONV - 1) + t + r0
            out = out + cw_ref[t:t + 1, :] * u_ref[o:o + rc, :]
        return out

    for r0, (g, v) in zip(starts, ups):
        ug_ref[8 + r0:8 + r0 + rc, :] = g
        uv_ref[8 + r0:8 + r0 + rc, :] = v
        act = (_silu(conv(ug_ref, cwg_ref, cbg_ref, r0)) * conv(uv_ref, cwv_ref, cbv_ref, r0)).astype(BF16)
        o_ref[r0:r0 + rc, :] += jnp.dot(act, wd_ref[...], preferred_element_type=F32)

    if final:
        @pl.when(j == pl.num_programs(1) - 1)
        def _():
            o_ref[...] = _rms(o_ref[...], fn_ref[...])


def _ffn(x2, ln, w_up, conv_w, conv_b, w_down, fn, *, seq, final):
    T, D = x2.shape
    dff = w_down.shape[0]
    tm = _row_tile(seq, 1024)
    tf = 512
    assert dff % tf == 0 and w_up.shape == (D, 2 * dff)
    nsb = seq // tm
    nf = dff // tf
    return pl.pallas_call(
        functools.partial(_ffn_kernel, nsb=nsb, final=final),
        grid=(T // tm, nf),
        in_specs=[
            pl.BlockSpec((tm, D), lambda i, j: (i, 0)),
            pl.BlockSpec((1, D), lambda i, j: (0, 0)),
            pl.BlockSpec((D, tf), lambda i, j: (0, j)),
            pl.BlockSpec((D, tf), lambda i, j: (0, nf + j)),
            pl.BlockSpec((FFN_CONV, tf), lambda i, j: (0, j)),
            pl.BlockSpec((FFN_CONV, tf), lambda i, j: (0, nf + j)),
            pl.BlockSpec((1, tf), lambda i, j: (0, j)),
            pl.BlockSpec((1, tf), lambda i, j: (0, nf + j)),
            pl.BlockSpec((tf, D), lambda i, j: (j, 0)),
            pl.BlockSpec((1, D), lambda i, j: (0, 0)),
        ],
        out_specs=pl.BlockSpec((tm, D), lambda i, j: (i, 0)),
        out_shape=jax.ShapeDtypeStruct((T, D), F32),
        scratch_shapes=[
            pltpu.VMEM((tm, D), BF16),
            pltpu.VMEM((nf, 2, 8, tf), F32),
            pltpu.VMEM((tm + 8, tf), F32),
            pltpu.VMEM((tm + 8, tf), F32),
        ],
        compiler_params=_cparams("arbitrary", "arbitrary"),
        name="ffn_final" if final else "ffn",
    )(x2, ln, w_up, w_up, conv_w, conv_w, conv_b, conv_b, w_down, fn)


def _rope_tables(seq):
    half = ATTN_HEAD_DIM // 2
    inv_freq = jnp.power(ROPE_THETA, -jnp.arange(half, dtype=F32) / half)
    ang = jnp.arange(seq, dtype=F32)[:, None] * inv_freq[None, :]
    cos, sin = jnp.cos(ang), jnp.sin(ang)
    return jnp.concatenate([cos, cos], axis=-1), jnp.concatenate([-sin, sin], axis=-1)


def _chunk_constants():
    L = SSD_CHUNK
    low = np.tril(np.ones((L, L), np.float32))
    tri = jnp.asarray(np.where(low > 0, 0.0, NEG_INF).astype(np.float32))
    ltri = jnp.asarray(low).astype(BF16)
    return tri, ltri


def _block_onehot(seq):
    oh = np.zeros((seq, LANES), np.float32)
    oh[np.arange(seq), np.arange(seq) // MOBA_BLOCK] = 1.0
    return jnp.asarray(oh).astype(BF16)


def _pad_lanes(v):
    return jnp.pad(v, (0, LANES - v.shape[0]))[None, :]


def kernel(x, ln1, w_in, attn_norm, ssm_conv_w, ssm_conv_b, dt_bias, a_log, d_skip, ssm_norm, w_out, ln2, w_up, ffn_conv_w, ffn_conv_b, w_down, final_norm):
    batch, seq, d_model = x.shape
    depth = ln1.shape[0]
    attn_w = attn_norm.shape[1]
    ssm_w = ssm_norm.shape[1]
    xbc_w = ssm_conv_w.shape[2]
    n_ssm_heads = a_log.shape[1]
    assert ssm_w // n_ssm_heads == SSM_HEAD_DIM and n_ssm_heads <= LANES
    main_w = 3 * attn_w + ssm_w + xbc_w
    cos_t, sin_t = _rope_tables(seq)
    tri, ltri = _chunk_constants()
    onehot = _block_onehot(seq)

    x2 = x.reshape(batch * seq, d_model)
    for i in range(depth):
        w_main = w_in[i].astype(BF16)
        w_dt = jnp.pad(w_main[:, main_w:], ((0, 0), (0, LANES - n_ssm_heads)))
        q, k, v, z2, xbc2, dt2 = _proj_in(
            x2, ln1[i][None, :], w_main, w_dt, cos_t, sin_t, ssm_conv_w[i], ssm_conv_b[i][None, :],
            batch=batch, seq=seq, attn_w=attn_w, ssm_w=ssm_w, xbc_w=xbc_w)

        attn = _moba(q, k, v, onehot)

        y3 = _ssd(
            xbc2.reshape(batch, seq, xbc_w), dt2.reshape(batch, seq, LANES),
            z2.reshape(batch, seq, ssm_w),
            _pad_lanes(dt_bias[i]), _pad_lanes(a_log[i]),
            jnp.repeat(d_skip[i], SSM_HEAD_DIM)[None, :], ssm_norm[i][None, :], tri, ltri,
            ssm_w=ssm_w)

        x2 = _proj_out(attn, y3.reshape(batch * seq, ssm_w), attn_norm[i][None, :],
                       w_out[i].astype(BF16), x2, seq=seq)

        x2 = _ffn(x2, ln2[i][None, :], w_up[i].astype(BF16), ffn_conv_w[i], ffn_conv_b[i][None, :],
                  w_down[i].astype(BF16), final_norm[None, :], seq=seq, final=(i == depth - 1))
    return x2.reshape(batch, seq, d_model)
```

```python
import functools

import numpy as np
import jax
import jax.numpy as jnp
from jax import lax
from jax.experimental import pallas as pl
from jax.experimental.pallas import tpu as pltpu

F32 = jnp.float32
BF16 = jnp.bfloat16

NORM_EPS = 1e-6
NEG_INF = -1e30
LOG2E = 1.4426950408889634
ROPE_THETA = 10000.0

ATTN_HEAD_DIM = 128
MOBA_BLOCK = 256
MOBA_TOPK = 3
MOBA_MAX_BLOCKS = 8
MOBA_HEADS_PER_STEP = 2

SSM_HEAD_DIM = 64
SSM_GROUPS = 8
SSM_STATE = 128
SSM_CONV = 4
SSD_CHUNK = 256
SSD_GROUPS_PER_STEP = 8
FFN_CONV = 3
ROW_CHUNK = 512
FFN_ROW_CHUNK = 512

LANES = 128
BF16_ROWS = 16
VMEM_LIMIT_BYTES = 56 * 1024 * 1024


def _cparams(*sem):
    return pltpu.CompilerParams(dimension_semantics=sem, vmem_limit_bytes=VMEM_LIMIT_BYTES)


def _row_tile(seq, pref):
    t = min(seq, pref)
    assert seq % t == 0
    return t


def _rms(xf, g):
    ms = jnp.mean(xf * xf, axis=-1, keepdims=True)
    return xf * lax.rsqrt(ms + NORM_EPS) * g


def _silu(x):
    return x * (1.0 / (1.0 + jnp.exp(-x)))


def _nt_dot(a, b):
    return lax.dot_general(a, b, (((1,), (1,)), ((), ())), preferred_element_type=F32)


def _proj_in_kernel(x_ref, ln_ref, w_ref, wdt_ref, cos_ref, sin_ref, cw_ref, cb_ref,
                    q_ref, k_ref, v_ref, z_ref, xbc_ref, dt_ref,
                    h_ref, halo_ref, *, nq, nz, hpt, nsb):
    i = pl.program_id(0)
    j = pl.program_id(1)
    tm = x_ref.shape[0]

    @pl.when(j == 0)
    def _():
        h = _rms(x_ref[...], ln_ref[...]).astype(BF16)
        h_ref[...] = h
        dt_ref[...] = jnp.dot(h, wdt_ref[...], preferred_element_type=F32)

    rc = min(tm, ROW_CHUNK)
    starts = range(0, tm, rc)

    def mm_chunks():
        return [jnp.dot(h_ref[r0:r0 + rc, :], w_ref[...], preferred_element_type=F32) for r0 in starts]

    def rope_store(o_ref):
        accs = mm_chunks()
        for r0, acc in zip(starts, accs):
            cos = cos_ref[r0:r0 + rc, :]
            sin = sin_ref[r0:r0 + rc, :]
            for hh in range(hpt):
                a = acc[:, hh * LANES:(hh + 1) * LANES]
                o_ref[0, hh, r0:r0 + rc, :] = (
                    a * cos + pltpu.roll(a, ATTN_HEAD_DIM // 2, axis=1) * sin).astype(BF16)

    @pl.when(j < nq)
    def _():
        rope_store(q_ref)

    @pl.when((j >= nq) & (j < 2 * nq))
    def _():
        rope_store(k_ref)

    @pl.when((j >= 2 * nq) & (j < 3 * nq))
    def _():
        accs = mm_chunks()
        for r0, acc in zip(starts, accs):
            for hh in range(hpt):
                v_ref[0, hh, r0:r0 + rc, :] = acc[:, hh * LANES:(hh + 1) * LANES].astype(BF16)

    @pl.when((j >= 3 * nq) & (j < 3 * nq + nz))
    def _():
        accs = mm_chunks()
        for r0, acc in zip(starts, accs):
            z_ref[r0:r0 + rc, :] = _silu(acc).astype(BF16)

    @pl.when((i == 0) & (j == 0))
    def _():
        halo_ref[...] = jnp.zeros_like(halo_ref)

    @pl.when(j >= 3 * nq + nz)
    def _():
        jx = j - (3 * nq + nz)
        prev = jnp.where(i % nsb == 0, 0.0, halo_ref[jx])
        cw = cw_ref[...]
        row = lax.broadcasted_iota(jnp.int32, prev.shape, 0)
        accs = mm_chunks()
        for r0, acc in zip(starts, accs):
            conv = cb_ref[...] + cw[SSM_CONV - 1:SSM_CONV, :] * acc
            for sh in range(1, SSM_CONV):
                r = pltpu.roll(acc, sh, axis=0)
                head = jnp.where(row < sh, pltpu.roll(prev, sh, axis=0), r[0:8])
                xk = jnp.concatenate([head, r[8:]], axis=0)
                conv = conv + cw[SSM_CONV - 1 - sh:SSM_CONV - sh, :] * xk
            xbc_ref[r0:r0 + rc, :] = _silu(conv).astype(BF16)
            prev = acc[rc - 8:, :]
        halo_ref[jx] = prev


def _proj_in(x2, ln, w_main, w_dt, cos_t, sin_t, conv_w, conv_b, *, batch, seq, attn_w, ssm_w, xbc_w):
    T, D = x2.shape
    tm = _row_tile(seq, 1024)
    tn = 512
    nsb = seq // tm
    n_heads = attn_w // ATTN_HEAD_DIM
    hpt = tn // ATTN_HEAD_DIM
    nq = attn_w // tn
    nz = ssm_w // tn
    nx = xbc_w // tn
    nj = 3 * nq + nz + nx
    assert w_main.shape[0] == D and w_main.shape[1] >= nj * tn

    def clampj(lo, n):
        return lambda j: jnp.clip(j - lo, 0, n - 1)

    qj, kj, vj = clampj(0, nq), clampj(nq, nq), clampj(2 * nq, nq)
    zj, xj = clampj(3 * nq, nz), clampj(3 * nq + nz, nx)

    head_shape = jax.ShapeDtypeStruct((batch, n_heads, seq, ATTN_HEAD_DIM), BF16)

    def head_spec(fj):
        return pl.BlockSpec((1, hpt, tm, ATTN_HEAD_DIM), lambda i, j: (i // nsb, fj(j), i % nsb, 0))

    return pl.pallas_call(
        functools.partial(_proj_in_kernel, nq=nq, nz=nz, hpt=hpt, nsb=nsb),
        grid=(T // tm, nj),
        in_specs=[
            pl.BlockSpec((tm, D), lambda i, j: (i, 0)),
            pl.BlockSpec((1, D), lambda i, j: (0, 0)),
            pl.BlockSpec((D, tn), lambda i, j: (0, j)),
            pl.BlockSpec((D, LANES), lambda i, j: (0, 0)),
            pl.BlockSpec((tm, ATTN_HEAD_DIM), lambda i, j: (i % nsb, 0)),
            pl.BlockSpec((tm, ATTN_HEAD_DIM), lambda i, j: (i % nsb, 0)),
            pl.BlockSpec((SSM_CONV, tn), lambda i, j: (0, xj(j))),
            pl.BlockSpec((1, tn), lambda i, j: (0, xj(j))),
        ],
        out_specs=[
            head_spec(qj), head_spec(kj), head_spec(vj),
            pl.BlockSpec((tm, tn), lambda i, j: (i, zj(j))),
            pl.BlockSpec((tm, tn), lambda i, j: (i, xj(j))),
            pl.BlockSpec((tm, LANES), lambda i, j: (i, 0)),
        ],
        out_shape=[
            head_shape, head_shape, head_shape,
            jax.ShapeDtypeStruct((T, ssm_w), BF16),
            jax.ShapeDtypeStruct((T, xbc_w), BF16),
            jax.ShapeDtypeStruct((T, LANES), F32),
        ],
        scratch_shapes=[
            pltpu.VMEM((tm, D), BF16),
            pltpu.VMEM((nx, 8, tn), F32),
        ],
        compiler_params=_cparams("arbitrary", "arbitrary"),
        name="proj_in",
    )(x2, ln, w_main, w_dt, cos_t, sin_t, conv_w, conv_b)


def _moba_kernel(q_ref, k_ref, v_ref, oh_ref, o_ref, qa_ref, *, nb, hps, scale):
    seq = nb * MOBA_BLOCK
    nbp = MOBA_MAX_BLOCKS
    blk = MOBA_BLOCK
    H = range(hps)

    own = jnp.right_shift(lax.broadcasted_iota(jnp.int32, (nbp, seq), 1), blk.bit_length() - 1)
    kb = lax.broadcasted_iota(jnp.int32, (nbp, seq), 0)
    for u in H:
        kf = k_ref[0, u].astype(F32)
        rows = [jnp.sum(kf[b * blk:(b + 1) * blk], axis=0, keepdims=True) for b in range(nb)]
        if nb < nbp:
            rows.append(jnp.zeros((nbp - nb, ATTN_HEAD_DIM), F32))
        kmean = jnp.concatenate(rows, axis=0) * (1.0 / blk)
        k_hi = kmean.astype(BF16)
        k_lo = (kmean - k_hi.astype(F32)).astype(BF16)
        q = q_ref[0, u]
        g2 = _nt_dot(jnp.concatenate([k_hi, k_lo], axis=0), q)
        gate = g2[0:nbp] + g2[nbp:2 * nbp]
        rank = jnp.zeros((nbp, seq), F32)
        for b in range(nb):
            gb = gate[b:b + 1, :]
            beats = (b < own) & ((gb > gate) | ((gb == gate) & (b < kb)))
            rank = rank + jnp.where(beats, 1.0, 0.0)
        allowed = ((kb < own) & (rank < MOBA_TOPK)) | (kb == own)
        bias_t = jnp.where(allowed, 0.0, NEG_INF)
        bias_t = jnp.concatenate([bias_t, jnp.zeros((LANES - nbp, seq), F32)], axis=0)
        qa_ref[u, :, 0:ATTN_HEAD_DIM] = q
        qa_ref[u, :, ATTN_HEAD_DIM:] = bias_t.T.astype(BF16)

    qi = lax.broadcasted_iota(jnp.int32, (blk, blk), 0)
    ki = lax.broadcasted_iota(jnp.int32, (blk, blk), 1)
    causal = ki <= qi
    c2 = scale * LOG2E

    def scores(u, i):
        hi = (i + 1) * blk
        ka = jnp.concatenate([k_ref[0, u, 0:hi, :], oh_ref[0:hi, :]], axis=1)
        return _nt_dot(qa_ref[u, i * blk:hi, :], ka)

    s_next = [scores(u, 0) for u in H]
    for i in range(nb):
        hi = (i + 1) * blk
        s_cur = s_next
        if i + 1 < nb:
            s_next = [scores(u, i + 1) for u in H]
        for u in H:
            s = s_cur[u]
            s_own = jnp.where(causal, s[:, i * blk:], NEG_INF)
            s = jnp.concatenate([s[:, 0:i * blk], s_own], axis=1) if i else s_own
            m = jnp.max(s, axis=1, keepdims=True)
            p = jnp.exp2((s - m) * c2)
            l = jnp.sum(p, axis=1, keepdims=True)
            acc = jnp.dot(p.astype(BF16), v_ref[0, u, 0:hi, :], preferred_element_type=F32)
            o_ref[0, u, i * blk:hi, :] = (acc / l).astype(BF16)


def _moba(q, k, v, onehot):
    batch, n_heads, seq, dh = q.shape
    nb = seq // MOBA_BLOCK
    hps = MOBA_HEADS_PER_STEP
    assert seq % MOBA_BLOCK == 0 and nb <= MOBA_MAX_BLOCKS and dh == ATTN_HEAD_DIM and n_heads % hps == 0
    full = pl.BlockSpec((1, hps, seq, dh), lambda b, h: (b, h, 0, 0))
    return pl.pallas_call(
        functools.partial(_moba_kernel, nb=nb, hps=hps, scale=dh ** -0.5),
        grid=(batch, n_heads // hps),
        in_specs=[full, full, full, pl.BlockSpec((seq, LANES), lambda b, h: (0, 0))],
        out_specs=full,
        out_shape=jax.ShapeDtypeStruct((batch, n_heads, seq, dh), BF16),
        scratch_shapes=[pltpu.VMEM((hps, seq, 2 * dh), BF16)],
        compiler_params=_cparams("parallel", "parallel"),
        name="moba",
    )(q, k, v, onehot)


def _ssd_kernel(xx_ref, xb_ref, xc_ref, dt_ref, dtb_ref, alog_ref, dskip_ref, z_ref, nw_ref,
                tri_ref, ltri_ref, y_ref,
                st_ref, acsc_ref, rows_ref, rowt_ref, *, hpg, gps):
    c = pl.program_id(1)
    gp = pl.program_id(2)
    L = SSD_CHUNK
    P = SSM_HEAD_DIM
    N = SSM_STATE
    gw = hpg * P
    n_groups = st_ref.shape[0]

    @pl.when(gp == 0)
    def _():
        dtv = dt_ref[0] + dtb_ref[...]
        dtv = jnp.maximum(dtv, 0.0) + jnp.log1p(jnp.exp(-jnp.abs(dtv)))
        a = dtv * (-jnp.exp(alog_ref[...]))
        a_hi = a.astype(BF16)
        r1 = a - a_hi.astype(F32)
        a_mid = r1.astype(BF16)
        a_lo = (r1 - a_mid.astype(F32)).astype(BF16)
        cs3 = jnp.dot(ltri_ref[...], jnp.concatenate([a_hi, a_mid, a_lo], axis=1),
                      preferred_element_type=F32)
        acs2 = (cs3[:, 0:LANES] + cs3[:, LANES:2 * LANES] + cs3[:, 2 * LANES:]) * LOG2E
        rowt_ref[...] = (acs2 - jnp.log2(dtv)).T
        for gg in range(n_groups):
            sh = (LANES - hpg * gg) % LANES
            acsc_ref[gg] = pltpu.roll(acs2, sh, axis=1) if sh else acs2
            rows_ref[gg, 0:hpg, :] = rowt_ref[hpg * gg:hpg * (gg + 1), :]

    @pl.when(c == 0)
    def _():
        for u in range(gps):
            st_ref[gp * gps + u] = jnp.zeros((N, gw), F32)

    tri = tri_ref[...]
    lane_head = jnp.right_shift(lax.broadcasted_iota(jnp.int32, (1, gw), 1), P.bit_length() - 1)

    U = range(gps)
    gs = [gp * gps + u for u in U]
    xs_b = [xx_ref[0, :, u * gw:(u + 1) * gw] for u in U]
    bm_b = [xb_ref[0, :, u * N:(u + 1) * N] for u in U]
    cm_b = [xc_ref[0, :, u * N:(u + 1) * N] for u in U]
    cb_mat = [_nt_dot(cm_b[u], bm_b[u]) for u in U]
    bm_t = [bm_b[u].astype(F32).T for u in U]
    acsc = [acsc_ref[gs[u]] for u in U]
    e_col = [jnp.exp2(acsc[u]) for u in U]
    a_end = [acsc[u][L - 1:L, :] for u in U]

    mp = [[] for _ in U]
    bw = [[] for _ in U]
    xm = [[] for _ in U]
    dfs = [jnp.zeros((L, gw), F32) for _ in U]
    dch = [jnp.zeros((1, gw), F32) for _ in U]
    for r in range(hpg):
        hmask = lane_head == r
        for u in U:
            a_col = acsc[u][:, r:r + 1]
            a_row = rows_ref[gs[u], r:r + 1, :]
            a_last = a_end[u][:, r:r + 1]
            mp[u].append((cb_mat[u] * jnp.exp2(a_col - a_row + tri)).astype(BF16))
            bw[u].append((bm_t[u] * jnp.exp2(a_last - a_row)).astype(BF16))
            xm[u].append(jnp.where(hmask, xs_b[u], jnp.zeros_like(xs_b[u])))
            dfs[u] = jnp.where(hmask, jnp.broadcast_to(e_col[u][:, r:r + 1], (L, gw)), dfs[u])
            dch[u] = jnp.where(hmask, jnp.broadcast_to(jnp.exp2(a_last), (1, gw)), dch[u])
    mp = [jnp.concatenate(mp[u], axis=1) for u in U]
    bw = [jnp.concatenate(bw[u], axis=1) for u in U]
    xm = [jnp.concatenate(xm[u], axis=0) for u in U]

    st_old = [st_ref[gs[u]] for u in U]
    y = [jnp.dot(mp[u], xm[u], preferred_element_type=F32) for u in U]
    yo = [jnp.dot(cm_b[u], st_old[u].astype(BF16), preferred_element_type=F32) for u in U]
    sn = [jnp.dot(bw[u], xm[u], preferred_element_type=F32) for u in U]
    for u in U:
        st_ref[gs[u]] = st_old[u] * dch[u] + sn[u]
    for u in U:
        cols = slice(u * gw, (u + 1) * gw)
        yy = y[u] + yo[u] * dfs[u] + dskip_ref[:, cols] * xs_b[u].astype(F32)
        hg = yy * z_ref[0, :, cols].astype(F32)
        y_ref[0, :, cols] = (_rms(hg, nw_ref[:, cols])).astype(BF16)


def _ssd(xbc3, dt3, z3, dtb, alog, dskip, nw, tri, ltri, *, ssm_w):
    batch, seq, xbc_w = xbc3.shape
    L = SSD_CHUNK
    N = SSM_STATE
    G = SSM_GROUPS
    gps = SSD_GROUPS_PER_STEP
    assert seq % L == 0 and G % gps == 0
    nc = seq // L
    gw = ssm_w // G
    hpg = gw // SSM_HEAD_DIM
    assert hpg <= 8 and xbc_w == ssm_w + 2 * G * N and gw % LANES == 0
    b_off = ssm_w // (gps * N)
    c_off = b_off + G // gps

    return pl.pallas_call(
        functools.partial(_ssd_kernel, hpg=hpg, gps=gps),
        grid=(batch, nc, G // gps),
        in_specs=[
            pl.BlockSpec((1, L, gps * gw), lambda b, c, g: (b, c, g)),
            pl.BlockSpec((1, L, gps * N), lambda b, c, g: (b, c, b_off + g)),
            pl.BlockSpec((1, L, gps * N), lambda b, c, g: (b, c, c_off + g)),
            pl.BlockSpec((1, L, LANES), lambda b, c, g: (b, c, 0)),
            pl.BlockSpec((1, LANES), lambda b, c, g: (0, 0)),
            pl.BlockSpec((1, LANES), lambda b, c, g: (0, 0)),
            pl.BlockSpec((1, gps * gw), lambda b, c, g: (0, g)),
            pl.BlockSpec((1, L, gps * gw), lambda b, c, g: (b, c, g)),
            pl.BlockSpec((1, gps * gw), lambda b, c, g: (0, g)),
            pl.BlockSpec((L, L), lambda b, c, g: (0, 0)),
            pl.BlockSpec((L, L), lambda b, c, g: (0, 0)),
        ],
        out_specs=pl.BlockSpec((1, L, gps * gw), lambda b, c, g: (b, c, g)),
        out_shape=jax.ShapeDtypeStruct((batch, seq, ssm_w), BF16),
        scratch_shapes=[
            pltpu.VMEM((G, N, gw), F32),
            pltpu.VMEM((G, L, LANES), F32),
            pltpu.VMEM((G, 8, L), F32),
            pltpu.VMEM((LANES, L), F32),
        ],
        compiler_params=_cparams("parallel", "arbitrary", "arbitrary"),
        name="ssd",
    )(xbc3, xbc3, xbc3, dt3, dtb, alog, dskip, z3, nw, tri, ltri)


def _proj_out_kernel(attn_ref, y_ref, an_ref, w_ref, x_ref, o_ref, cat_ref, *, n_heads):
    j = pl.program_id(1)
    aw = n_heads * ATTN_HEAD_DIM

    @pl.when(j == 0)
    def _():
        a = jnp.concatenate([attn_ref[0, hh].astype(F32) for hh in range(n_heads)], axis=1)
        cat_ref[:, 0:aw] = _rms(a, an_ref[...]).astype(BF16)
        cat_ref[:, aw:] = y_ref[...]

    tm = x_ref.shape[0]
    rc = min(tm, ROW_CHUNK)
    starts = range(0, tm, rc)
    accs = [jnp.dot(cat_ref[r0:r0 + rc, :], w_ref[...], preferred_element_type=F32) for r0 in starts]
    for r0, acc in zip(starts, accs):
        o_ref[r0:r0 + rc, :] = x_ref[r0:r0 + rc, :] + acc


def _proj_out(attn, y2, an, w_out, x2, *, seq):
    batch, n_heads, _, dh = attn.shape
    T, D = x2.shape
    aw = n_heads * dh
    sw = y2.shape[1]
    tm = _row_tile(seq, 1024)
    tn = 512
    nsb = seq // tm
    return pl.pallas_call(
        functools.partial(_proj_out_kernel, n_heads=n_heads),
        grid=(T // tm, D // tn),
        in_specs=[
            pl.BlockSpec((1, n_heads, tm, dh), lambda i, j: (i // nsb, 0, i % nsb, 0)),
            pl.BlockSpec((tm, sw), lambda i, j: (i, 0)),
            pl.BlockSpec((1, aw), lambda i, j: (0, 0)),
            pl.BlockSpec((aw + sw, tn), lambda i, j: (0, j)),
            pl.BlockSpec((tm, tn), lambda i, j: (i, j)),
        ],
        out_specs=pl.BlockSpec((tm, tn), lambda i, j: (i, j)),
        out_shape=jax.ShapeDtypeStruct((T, D), F32),
        scratch_shapes=[pltpu.VMEM((tm, aw + sw), BF16)],
        compiler_params=_cparams("parallel", "arbitrary"),
        name="proj_out",
    )(attn, y2, an, w_out, x2)


def _ffn_kernel(x_ref, ln_ref, wg_ref, wv_ref, cwg_ref, cwv_ref, cbg_ref, cbv_ref, wd_ref, fn_ref, o_ref,
                h_ref, halo_ref, ug_ref, uv_ref, *, nsb, final):
    i = pl.program_id(0)
    j = pl.program_id(1)
    tm = x_ref.shape[0]

    @pl.when((i == 0) & (j == 0))
    def _():
        halo_ref[...] = jnp.zeros_like(halo_ref)

    @pl.when(j == 0)
    def _():
        x = x_ref[...]
        h_ref[...] = _rms(x, ln_ref[...]).astype(BF16)
        o_ref[...] = x

    first = i % nsb == 0
    rc = min(tm, FFN_ROW_CHUNK)
    starts = range(0, tm, rc)
    ups = [(jnp.dot(h_ref[r0:r0 + rc, :], wg_ref[...], preferred_element_type=F32),
            jnp.dot(h_ref[r0:r0 + rc, :], wv_ref[...], preferred_element_type=F32)) for r0 in starts]
    ug_ref[0:8, :] = jnp.where(first, 0.0, halo_ref[j, 0])
    uv_ref[0:8, :] = jnp.where(first, 0.0, halo_ref[j, 1])
    halo_ref[j, 0] = ups[-1][0][rc - 8:, :]
    halo_ref[j, 1] = ups[-1][1][rc - 8:, :]

    def conv(u_ref, cw_ref, cb_ref, r0):
        out = cb_ref[...]
        for t in range(FFN_CONV):
            o = 8 - (FFN_CONV - 1) + t + r0
            out = out + cw_ref[t:t + 1, :] * u_ref[o:o + rc, :]
        return out

    for r0, (g, v) in zip(starts, ups):
        ug_ref[8 + r0:8 + r0 + rc, :] = g
        uv_ref[8 + r0:8 + r0 + rc, :] = v
        act = (_silu(conv(ug_ref, cwg_ref, cbg_ref, r0)) * conv(uv_ref, cwv_ref, cbv_ref, r0)).astype(BF16)
        o_ref[r0:r0 + rc, :] += jnp.dot(act, wd_ref[...], preferred_element_type=F32)

    if final:
        @pl.when(j == pl.num_programs(1) - 1)
        def _():
            o_ref[...] = _rms(o_ref[...], fn_ref[...])


def _ffn(x2, ln, w_up, conv_w, conv_b, w_down, fn, *, seq, final):
    T, D = x2.shape
    dff = w_down.shape[0]
    tm = _row_tile(seq, 1024)
    tf = 512
    assert dff % tf == 0 and w_up.shape == (D, 2 * dff)
    nsb = seq // tm
    nf = dff // tf
    return pl.pallas_call(
        functools.partial(_ffn_kernel, nsb=nsb, final=final),
        grid=(T // tm, nf),
        in_specs=[
            pl.BlockSpec((tm, D), lambda i, j: (i, 0)),
            pl.BlockSpec((1, D), lambda i, j: (0, 0)),
            pl.BlockSpec((D, tf), lambda i, j: (0, j)),
            pl.BlockSpec((D, tf), lambda i, j: (0, nf + j)),
            pl.BlockSpec((FFN_CONV, tf), lambda i, j: (0, j)),
            pl.BlockSpec((FFN_CONV, tf), lambda i, j: (0, nf + j)),
            pl.BlockSpec((1, tf), lambda i, j: (0, j)),
            pl.BlockSpec((1, tf), lambda i, j: (0, nf + j)),
            pl.BlockSpec((tf, D), lambda i, j: (j, 0)),
            pl.BlockSpec((1, D), lambda i, j: (0, 0)),
        ],
        out_specs=pl.BlockSpec((tm, D), lambda i, j: (i, 0)),
        out_shape=jax.ShapeDtypeStruct((T, D), F32),
        scratch_shapes=[
            pltpu.VMEM((tm, D), BF16),
            pltpu.VMEM((nf, 2, 8, tf), F32),
            pltpu.VMEM((tm + 8, tf), F32),
            pltpu.VMEM((tm + 8, tf), F32),
        ],
        compiler_params=_cparams("arbitrary", "arbitrary"),
        name="ffn_final" if final else "ffn",
    )(x2, ln, w_up, w_up, conv_w, conv_w, conv_b, conv_b, w_down, fn)


def _rope_tables(seq):
    half = ATTN_HEAD_DIM // 2
    inv_freq = jnp.power(ROPE_THETA, -jnp.arange(half, dtype=F32) / half)
    ang = jnp.arange(seq, dtype=F32)[:, None] * inv_freq[None, :]
    cos, sin = jnp.cos(ang), jnp.sin(ang)
    return jnp.concatenate([cos, cos], axis=-1), jnp.concatenate([-sin, sin], axis=-1)


def _chunk_constants():
    L = SSD_CHUNK
    low = np.tril(np.ones((L, L), np.float32))
    tri = jnp.asarray(np.where(low > 0, 0.0, NEG_INF).astype(np.float32))
    ltri = jnp.asarray(low).astype(BF16)
    return tri, ltri


def _block_onehot(seq):
    oh = np.zeros((seq, LANES), np.float32)
    oh[np.arange(seq), np.arange(seq) // MOBA_BLOCK] = 1.0
    return jnp.asarray(oh).astype(BF16)


def _pad_lanes(v):
    return jnp.pad(v, (0, LANES - v.shape[0]))[None, :]


def kernel(x, ln1, w_in, attn_norm, ssm_conv_w, ssm_conv_b, dt_bias, a_log, d_skip, ssm_norm, w_out, ln2, w_up, ffn_conv_w, ffn_conv_b, w_down, final_norm):
    batch, seq, d_model = x.shape
    depth = ln1.shape[0]
    attn_w = attn_norm.shape[1]
    ssm_w = ssm_norm.shape[1]
    xbc_w = ssm_conv_w.shape[2]
    n_ssm_heads = a_log.shape[1]
    assert ssm_w // n_ssm_heads == SSM_HEAD_DIM and n_ssm_heads <= LANES
    main_w = 3 * attn_w + ssm_w + xbc_w
    cos_t, sin_t = _rope_tables(seq)
    tri, ltri = _chunk_constants()
    onehot = _block_onehot(seq)

    x2 = x.reshape(batch * seq, d_model)
    for i in range(depth):
        w_main = w_in[i].astype(BF16)
        w_dt = jnp.pad(w_main[:, main_w:], ((0, 0), (0, LANES - n_ssm_heads)))
        q, k, v, z2, xbc2, dt2 = _proj_in(
            x2, ln1[i][None, :], w_main, w_dt, cos_t, sin_t, ssm_conv_w[i], ssm_conv_b[i][None, :],
            batch=batch, seq=seq, attn_w=attn_w, ssm_w=ssm_w, xbc_w=xbc_w)

        attn = _moba(q, k, v, onehot)

        y3 = _ssd(
            xbc2.reshape(batch, seq, xbc_w), dt2.reshape(batch, seq, LANES),
            z2.reshape(batch, seq, ssm_w),
            _pad_lanes(dt_bias[i]), _pad_lanes(a_log[i]),
            jnp.repeat(d_skip[i], SSM_HEAD_DIM)[None, :], ssm_norm[i][None, :], tri, ltri,
            ssm_w=ssm_w)

        x2 = _proj_out(attn, y3.reshape(batch * seq, ssm_w), attn_norm[i][None, :],
                       w_out[i].astype(BF16), x2, seq=seq)

        x2 = _ffn(x2, ln2[i][None, :], w_up[i].astype(BF16), ffn_conv_w[i], ffn_conv_b[i][None, :],
                  w_down[i].astype(BF16), final_norm[None, :], seq=seq, final=(i == depth - 1))
    return x2.reshape(batch, seq, d_model)
```

```python
import functools

import numpy as np
import jax
import jax.numpy as jnp
from jax import lax
from jax.experimental import pallas as pl
from jax.experimental.pallas import tpu as pltpu

F32 = jnp.float32
BF16 = jnp.bfloat16

NORM_EPS = 1e-6
NEG_INF = -1e30
LOG2E = 1.4426950408889634
ROPE_THETA = 10000.0

ATTN_HEAD_DIM = 128
MOBA_BLOCK = 256
MOBA_TOPK = 3
MOBA_MAX_BLOCKS = 8
MOBA_HEADS_PER_STEP = 2

SSM_HEAD_DIM = 64
SSM_GROUPS = 8
SSM_STATE = 128
SSM_CONV = 4
SSD_CHUNK = 256
SSD_GROUPS_PER_STEP = 8
FFN_CONV = 3
ROW_CHUNK = 512
FFN_ROW_CHUNK = 512

LANES = 128
BF16_ROWS = 16
VMEM_LIMIT_BYTES = 56 * 1024 * 1024


def _cparams(*sem):
    return pltpu.CompilerParams(dimension_semantics=sem, vmem_limit_bytes=VMEM_LIMIT_BYTES)


def _row_tile(seq, pref):
    t = min(seq, pref)
    assert seq % t == 0
    return t


def _rms(xf, g):
    ms = jnp.mean(xf * xf, axis=-1, keepdims=True)
    return xf * lax.rsqrt(ms + NORM_EPS) * g


def _silu(x):
    return x * (1.0 / (1.0 + jnp.exp(-x)))


def _nt_dot(a, b):
    return lax.dot_general(a, b, (((1,), (1,)), ((), ())), preferred_element_type=F32)


def _proj_in_kernel(x_ref, ln_ref, w_ref, wdt_ref, cos_ref, sin_ref, cw_ref, cb_ref,
                    q_ref, k_ref, v_ref, z_ref, xbc_ref, dt_ref,
                    h_ref, halo_ref, *, nq, nz, hpt, nsb):
    i = pl.program_id(0)
    j = pl.program_id(1)
    tm = x_ref.shape[0]

    @pl.when(j == 0)
    def _():
        h = _rms(x_ref[...], ln_ref[...]).astype(BF16)
        h_ref[...] = h
        dt_ref[...] = jnp.dot(h, wdt_ref[...], preferred_element_type=F32)

    rc = min(tm, ROW_CHUNK)
    starts = range(0, tm, rc)

    def mm_chunks():
        return [jnp.dot(h_ref[r0:r0 + rc, :], w_ref[...], preferred_element_type=F32) for r0 in starts]

    def rope_store(o_ref):
        accs = mm_chunks()
        for r0, acc in zip(starts, accs):
            cos = cos_ref[r0:r0 + rc, :]
            sin = sin_ref[r0:r0 + rc, :]
            for hh in range(hpt):
                a = acc[:, hh * LANES:(hh + 1) * LANES]
                o_ref[0, hh, r0:r0 + rc, :] = (
                    a * cos + pltpu.roll(a, ATTN_HEAD_DIM // 2, axis=1) * sin).astype(BF16)

    @pl.when(j < nq)
    def _():
        rope_store(q_ref)

    @pl.when((j >= nq) & (j < 2 * nq))
    def _():
        rope_store(k_ref)

    @pl.when((j >= 2 * nq) & (j < 3 * nq))
    def _():
        accs = mm_chunks()
        for r0, acc in zip(starts, accs):
            for hh in range(hpt):
                v_ref[0, hh, r0:r0 + rc, :] = acc[:, hh * LANES:(hh + 1) * LANES].astype(BF16)

    @pl.when((j >= 3 * nq) & (j < 3 * nq + nz))
    def _():
        accs = mm_chunks()
        for r0, acc in zip(starts, accs):
            z_ref[r0:r0 + rc, :] = _silu(acc).astype(BF16)

    @pl.when((i == 0) & (j == 0))
    def _():
        halo_ref[...] = jnp.zeros_like(halo_ref)

    @pl.when(j >= 3 * nq + nz)
    def _():
        jx = j - (3 * nq + nz)
        prev = jnp.where(i % nsb == 0, 0.0, halo_ref[jx])
        cw = cw_ref[...]
        row = lax.broadcasted_iota(jnp.int32, prev.shape, 0)
        accs = mm_chunks()
        for r0, acc in zip(starts, accs):
            conv = cb_ref[...] + cw[SSM_CONV - 1:SSM_CONV, :] * acc
            for sh in range(1, SSM_CONV):
                r = pltpu.roll(acc, sh, axis=0)
                head = jnp.where(row < sh, pltpu.roll(prev, sh, axis=0), r[0:8])
                xk = jnp.concatenate([head, r[8:]], axis=0)
                conv = conv + cw[SSM_CONV - 1 - sh:SSM_CONV - sh, :] * xk
            xbc_ref[r0:r0 + rc, :] = _silu(conv).astype(BF16)
            prev = acc[rc - 8:, :]
        halo_ref[jx] = prev


def _proj_in(x2, ln, w_main, w_dt, cos_t, sin_t, conv_w, conv_b, *, batch, seq, attn_w, ssm_w, xbc_w):
    T, D = x2.shape
    tm = _row_tile(seq, 1024)
    tn = 1024
    nsb = seq // tm
    n_heads = attn_w // ATTN_HEAD_DIM
    hpt = tn // ATTN_HEAD_DIM
    nq = attn_w // tn
    nz = ssm_w // tn
    nx = xbc_w // tn
    nj = 3 * nq + nz + nx
    assert w_main.shape[0] == D and w_main.shape[1] >= nj * tn

    def clampj(lo, n):
        return lambda j: jnp.clip(j - lo, 0, n - 1)

    qj, kj, vj = clampj(0, nq), clampj(nq, nq), clampj(2 * nq, nq)
    zj, xj = clampj(3 * nq, nz), clampj(3 * nq + nz, nx)

    head_shape = jax.ShapeDtypeStruct((batch, n_heads, seq, ATTN_HEAD_DIM), BF16)

    def head_spec(fj):
        return pl.BlockSpec((1, hpt, tm, ATTN_HEAD_DIM), lambda i, j: (i // nsb, fj(j), i % nsb, 0))

    return pl.pallas_call(
        functools.partial(_proj_in_kernel, nq=nq, nz=nz, hpt=hpt, nsb=nsb),
        grid=(T // tm, nj),
        in_specs=[
            pl.BlockSpec((tm, D), lambda i, j: (i, 0), pipeline_mode=pl.Buffered(1)),
            pl.BlockSpec((1, D), lambda i, j: (0, 0)),
            pl.BlockSpec((D, tn), lambda i, j: (0, j)),
            pl.BlockSpec((D, LANES), lambda i, j: (0, 0)),
            pl.BlockSpec((tm, ATTN_HEAD_DIM), lambda i, j: (i % nsb, 0)),
            pl.BlockSpec((tm, ATTN_HEAD_DIM), lambda i, j: (i % nsb, 0)),
            pl.BlockSpec((SSM_CONV, tn), lambda i, j: (0, xj(j))),
            pl.BlockSpec((1, tn), lambda i, j: (0, xj(j))),
        ],
        out_specs=[
            head_spec(qj), head_spec(kj), head_spec(vj),
            pl.BlockSpec((tm, tn), lambda i, j: (i, zj(j))),
            pl.BlockSpec((tm, tn), lambda i, j: (i, xj(j))),
            pl.BlockSpec((tm, LANES), lambda i, j: (i, 0)),
        ],
        out_shape=[
            head_shape, head_shape, head_shape,
            jax.ShapeDtypeStruct((T, ssm_w), BF16),
            jax.ShapeDtypeStruct((T, xbc_w), BF16),
            jax.ShapeDtypeStruct((T, LANES), F32),
        ],
        scratch_shapes=[
            pltpu.VMEM((tm, D), BF16),
            pltpu.VMEM((nx, 8, tn), F32),
        ],
        compiler_params=_cparams("arbitrary", "arbitrary"),
        name="proj_in",
    )(x2, ln, w_main, w_dt, cos_t, sin_t, conv_w, conv_b)


def _moba_kernel(q_ref, k_ref, v_ref, oh_ref, o_ref, qa_ref, *, nb, hps, scale):
    seq = nb * MOBA_BLOCK
    nbp = MOBA_MAX_BLOCKS
    blk = MOBA_BLOCK
    H = range(hps)

    own = jnp.right_shift(lax.broadcasted_iota(jnp.int32, (nbp, seq), 1), blk.bit_length() - 1)
    kb = lax.broadcasted_iota(jnp.int32, (nbp, seq), 0)
    for u in H:
        kf = k_ref[0, u].astype(F32)
        rows = [jnp.sum(kf[b * blk:(b + 1) * blk], axis=0, keepdims=True) for b in range(nb)]
        if nb < nbp:
            rows.append(jnp.zeros((nbp - nb, ATTN_HEAD_DIM), F32))
        kmean = jnp.concatenate(rows, axis=0) * (1.0 / blk)
        k_hi = kmean.astype(BF16)
        k_lo = (kmean - k_hi.astype(F32)).astype(BF16)
        q = q_ref[0, u]
        g2 = _nt_dot(jnp.concatenate([k_hi, k_lo], axis=0), q)
        gate = g2[0:nbp] + g2[nbp:2 * nbp]
        rank = jnp.zeros((nbp, seq), F32)
        for b in range(nb):
            gb = gate[b:b + 1, :]
            beats = (b < own) & ((gb > gate) | ((gb == gate) & (b < kb)))
            rank = rank + jnp.where(beats, 1.0, 0.0)
        allowed = ((kb < own) & (rank < MOBA_TOPK)) | (kb == own)
        bias_t = jnp.where(allowed, 0.0, NEG_INF)
        bias_t = jnp.concatenate([bias_t, jnp.zeros((LANES - nbp, seq), F32)], axis=0)
        qa_ref[u, :, 0:ATTN_HEAD_DIM] = q
        qa_ref[u, :, ATTN_HEAD_DIM:] = bias_t.T.astype(BF16)

    qi = lax.broadcasted_iota(jnp.int32, (blk, blk), 0)
    ki = lax.broadcasted_iota(jnp.int32, (blk, blk), 1)
    causal = ki <= qi
    c2 = scale * LOG2E

    def scores(u, i):
        hi = (i + 1) * blk
        ka = jnp.concatenate([k_ref[0, u, 0:hi, :], oh_ref[0:hi, :]], axis=1)
        return _nt_dot(qa_ref[u, i * blk:hi, :], ka)

    s_next = [scores(u, 0) for u in H]
    for i in range(nb):
        hi = (i + 1) * blk
        s_cur = s_next
        if i + 1 < nb:
            s_next = [scores(u, i + 1) for u in H]
        for u in H:
            s = s_cur[u]
            s_own = jnp.where(causal, s[:, i * blk:], NEG_INF)
            s = jnp.concatenate([s[:, 0:i * blk], s_own], axis=1) if i else s_own
            m = jnp.max(s, axis=1, keepdims=True)
            p = jnp.exp2((s - m) * c2)
            l = jnp.sum(p, axis=1, keepdims=True)
            acc = jnp.dot(p.astype(BF16), v_ref[0, u, 0:hi, :], preferred_element_type=F32)
            o_ref[0, u, i * blk:hi, :] = (acc / l).astype(BF16)


def _moba(q, k, v, onehot):
    batch, n_heads, seq, dh = q.shape
    nb = seq // MOBA_BLOCK
    hps = MOBA_HEADS_PER_STEP
    assert seq % MOBA_BLOCK == 0 and nb <= MOBA_MAX_BLOCKS and dh == ATTN_HEAD_DIM and n_heads % hps == 0
    full = pl.BlockSpec((1, hps, seq, dh), lambda b, h: (b, h, 0, 0))
    return pl.pallas_call(
        functools.partial(_moba_kernel, nb=nb, hps=hps, scale=dh ** -0.5),
        grid=(batch, n_heads // hps),
        in_specs=[full, full, full, pl.BlockSpec((seq, LANES), lambda b, h: (0, 0))],
        out_specs=full,
        out_shape=jax.ShapeDtypeStruct((batch, n_heads, seq, dh), BF16),
        scratch_shapes=[pltpu.VMEM((hps, seq, 2 * dh), BF16)],
        compiler_params=_cparams("parallel", "parallel"),
        name="moba",
    )(q, k, v, onehot)


def _ssd_kernel(xx_ref, xb_ref, xc_ref, dt_ref, dtb_ref, alog_ref, dskip_ref, z_ref, nw_ref,
                tri_ref, ltri_ref, y_ref,
                st_ref, acsc_ref, rows_ref, rowt_ref, *, hpg, gps):
    c = pl.program_id(1)
    gp = pl.program_id(2)
    L = SSD_CHUNK
    P = SSM_HEAD_DIM
    N = SSM_STATE
    gw = hpg * P
    n_groups = st_ref.shape[0]

    @pl.when(gp == 0)
    def _():
        dtv = dt_ref[0] + dtb_ref[...]
        dtv = jnp.maximum(dtv, 0.0) + jnp.log1p(jnp.exp(-jnp.abs(dtv)))
        a = dtv * (-jnp.exp(alog_ref[...]))
        a_hi = a.astype(BF16)
        r1 = a - a_hi.astype(F32)
        a_mid = r1.astype(BF16)
        a_lo = (r1 - a_mid.astype(F32)).astype(BF16)
        cs3 = jnp.dot(ltri_ref[...], jnp.concatenate([a_hi, a_mid, a_lo], axis=1),
                      preferred_element_type=F32)
        acs2 = (cs3[:, 0:LANES] + cs3[:, LANES:2 * LANES] + cs3[:, 2 * LANES:]) * LOG2E
        rowt_ref[...] = (acs2 - jnp.log2(dtv)).T
        for gg in range(n_groups):
            sh = (LANES - hpg * gg) % LANES
            acsc_ref[gg] = pltpu.roll(acs2, sh, axis=1) if sh else acs2
            rows_ref[gg, 0:hpg, :] = rowt_ref[hpg * gg:hpg * (gg + 1), :]

    @pl.when(c == 0)
    def _():
        for u in range(gps):
            st_ref[gp * gps + u] = jnp.zeros((N, gw), F32)

    tri = tri_ref[...]
    lane_head = jnp.right_shift(lax.broadcasted_iota(jnp.int32, (1, gw), 1), P.bit_length() - 1)

    U = range(gps)
    gs = [gp * gps + u for u in U]
    xs_b = [xx_ref[0, :, u * gw:(u + 1) * gw] for u in U]
    bm_b = [xb_ref[0, :, u * N:(u + 1) * N] for u in U]
    cm_b = [xc_ref[0, :, u * N:(u + 1) * N] for u in U]
    cb_mat = [_nt_dot(cm_b[u], bm_b[u]) for u in U]
    bm_t = [bm_b[u].astype(F32).T for u in U]
    acsc = [acsc_ref[gs[u]] for u in U]
    e_col = [jnp.exp2(acsc[u]) for u in U]
    a_end = [acsc[u][L - 1:L, :] for u in U]

    mp = [[] for _ in U]
    bw = [[] for _ in U]
    xm = [[] for _ in U]
    dfs = [jnp.zeros((L, gw), F32) for _ in U]
    dch = [jnp.zeros((1, gw), F32) for _ in U]
    for r in range(hpg):
        hmask = lane_head == r
        for u in U:
            a_col = acsc[u][:, r:r + 1]
            a_row = rows_ref[gs[u], r:r + 1, :]
            a_last = a_end[u][:, r:r + 1]
            mp[u].append((cb_mat[u] * jnp.exp2(a_col - a_row + tri)).astype(BF16))
            bw[u].append((bm_t[u] * jnp.exp2(a_last - a_row)).astype(BF16))
            xm[u].append(jnp.where(hmask, xs_b[u], jnp.zeros_like(xs_b[u])))
            dfs[u] = jnp.where(hmask, jnp.broadcast_to(e_col[u][:, r:r + 1], (L, gw)), dfs[u])
            dch[u] = jnp.where(hmask, jnp.broadcast_to(jnp.exp2(a_last), (1, gw)), dch[u])
    mp = [jnp.concatenate(mp[u], axis=1) for u in U]
    bw = [jnp.concatenate(bw[u], axis=1) for u in U]
    xm = [jnp.concatenate(xm[u], axis=0) for u in U]

    st_old = [st_ref[gs[u]] for u in U]
    y = [jnp.dot(mp[u], xm[u], preferred_element_type=F32) for u in U]
    yo = [jnp.dot(cm_b[u], st_old[u].astype(BF16), preferred_element_type=F32) for u in U]
    sn = [jnp.dot(bw[u], xm[u], preferred_element_type=F32) for u in U]
    for u in U:
        st_ref[gs[u]] = st_old[u] * dch[u] + sn[u]
    for u in U:
        cols = slice(u * gw, (u + 1) * gw)
        yy = y[u] + yo[u] * dfs[u] + dskip_ref[:, cols] * xs_b[u].astype(F32)
        hg = yy * z_ref[0, :, cols].astype(F32)
        y_ref[0, :, cols] = (_rms(hg, nw_ref[:, cols])).astype(BF16)


def _ssd(xbc3, dt3, z3, dtb, alog, dskip, nw, tri, ltri, *, ssm_w):
    batch, seq, xbc_w = xbc3.shape
    L = SSD_CHUNK
    N = SSM_STATE
    G = SSM_GROUPS
    gps = SSD_GROUPS_PER_STEP
    assert seq % L == 0 and G % gps == 0
    nc = seq // L
    gw = ssm_w // G
    hpg = gw // SSM_HEAD_DIM
    assert hpg <= 8 and xbc_w == ssm_w + 2 * G * N and gw % LANES == 0
    b_off = ssm_w // (gps * N)
    c_off = b_off + G // gps

    return pl.pallas_call(
        functools.partial(_ssd_kernel, hpg=hpg, gps=gps),
        grid=(batch, nc, G // gps),
        in_specs=[
            pl.BlockSpec((1, L, gps * gw), lambda b, c, g: (b, c, g)),
            pl.BlockSpec((1, L, gps * N), lambda b, c, g: (b, c, b_off + g)),
            pl.BlockSpec((1, L, gps * N), lambda b, c, g: (b, c, c_off + g)),
            pl.BlockSpec((1, L, LANES), lambda b, c, g: (b, c, 0)),
            pl.BlockSpec((1, LANES), lambda b, c, g: (0, 0)),
            pl.BlockSpec((1, LANES), lambda b, c, g: (0, 0)),
            pl.BlockSpec((1, gps * gw), lambda b, c, g: (0, g)),
            pl.BlockSpec((1, L, gps * gw), lambda b, c, g: (b, c, g)),
            pl.BlockSpec((1, gps * gw), lambda b, c, g: (0, g)),
            pl.BlockSpec((L, L), lambda b, c, g: (0, 0)),
            pl.BlockSpec((L, L), lambda b, c, g: (0, 0)),
        ],
        out_specs=pl.BlockSpec((1, L, gps * gw), lambda b, c, g: (b, c, g)),
        out_shape=jax.ShapeDtypeStruct((batch, seq, ssm_w), BF16),
        scratch_shapes=[
            pltpu.VMEM((G, N, gw), F32),
            pltpu.VMEM((G, L, LANES), F32),
            pltpu.VMEM((G, 8, L), F32),
            pltpu.VMEM((LANES, L), F32),
        ],
        compiler_params=_cparams("parallel", "arbitrary", "arbitrary"),
        name="ssd",
    )(xbc3, xbc3, xbc3, dt3, dtb, alog, dskip, z3, nw, tri, ltri)


def _proj_out_kernel(attn_ref, y_ref, an_ref, w_ref, x_ref, o_ref, cat_ref, *, n_heads):
    j = pl.program_id(1)
    aw = n_heads * ATTN_HEAD_DIM

    @pl.when(j == 0)
    def _():
        a = jnp.concatenate([attn_ref[0, hh].astype(F32) for hh in range(n_heads)], axis=1)
        cat_ref[:, 0:aw] = _rms(a, an_ref[...]).astype(BF16)
        cat_ref[:, aw:] = y_ref[...]

    tm = x_ref.shape[0]
    rc = min(tm, ROW_CHUNK)
    starts = range(0, tm, rc)
    accs = [jnp.dot(cat_ref[r0:r0 + rc, :], w_ref[...], preferred_element_type=F32) for r0 in starts]
    for r0, acc in zip(starts, accs):
        o_ref[r0:r0 + rc, :] = x_ref[r0:r0 + rc, :] + acc


def _proj_out(attn, y2, an, w_out, x2, *, seq):
    batch, n_heads, _, dh = attn.shape
    T, D = x2.shape
    aw = n_heads * dh
    sw = y2.shape[1]
    tm = _row_tile(seq, 1024)
    tn = 512
    nsb = seq // tm
    return pl.pallas_call(
        functools.partial(_proj_out_kernel, n_heads=n_heads),
        grid=(T // tm, D // tn),
        in_specs=[
            pl.BlockSpec((1, n_heads, tm, dh), lambda i, j: (i // nsb, 0, i % nsb, 0)),
            pl.BlockSpec((tm, sw), lambda i, j: (i, 0)),
            pl.BlockSpec((1, aw), lambda i, j: (0, 0)),
            pl.BlockSpec((aw + sw, tn), lambda i, j: (0, j)),
            pl.BlockSpec((tm, tn), lambda i, j: (i, j)),
        ],
        out_specs=pl.BlockSpec((tm, tn), lambda i, j: (i, j)),
        out_shape=jax.ShapeDtypeStruct((T, D), F32),
        scratch_shapes=[pltpu.VMEM((tm, aw + sw), BF16)],
        compiler_params=_cparams("parallel", "arbitrary"),
        name="proj_out",
    )(attn, y2, an, w_out, x2)


def _ffn_kernel(x_ref, ln_ref, wg_ref, wv_ref, cwg_ref, cwv_ref, cbg_ref, cbv_ref, wd_ref, fn_ref, o_ref,
                h_ref, halo_ref, ug_ref, uv_ref, *, nsb, final):
    i = pl.program_id(0)
    j = pl.program_id(1)
    tm = x_ref.shape[0]

    @pl.when((i == 0) & (j == 0))
    def _():
        halo_ref[...] = jnp.zeros_like(halo_ref)

    @pl.when(j == 0)
    def _():
        x = x_ref[...]
        h_ref[...] = _rms(x, ln_ref[...]).astype(BF16)
        o_ref[...] = x

    first = i % nsb == 0
    rc = min(tm, FFN_ROW_CHUNK)
    starts = range(0, tm, rc)
    ups = [(jnp.dot(h_ref[r0:r0 + rc, :], wg_ref[...], preferred_element_type=F32),
            jnp.dot(h_ref[r0:r0 + rc, :], wv_ref[...], preferred_element_type=F32)) for r0 in starts]
    ug_ref[0:8, :] = jnp.where(first, 0.0, halo_ref[j, 0])
    uv_ref[0:8, :] = jnp.where(first, 0.0, halo_ref[j, 1])
    halo_ref[j, 0] = ups[-1][0][rc - 8:, :]
    halo_ref[j, 1] = ups[-1][1][rc - 8:, :]

    def conv(u_ref, cw_ref, cb_ref, r0):
        out = cb_ref[...]
        for t in range(FFN_CONV):
            o = 8 - (FFN_CONV - 1) + t + r0
            out = out + cw_ref[t:t + 1, :] * u_ref[o:o + rc, :]
        return out

    for r0, (g, v) in zip(starts, ups):
        ug_ref[8 + r0:8 + r0 + rc, :] = g
        uv_ref[8 + r0:8 + r0 + rc, :] = v
        act = (_silu(conv(ug_ref, cwg_ref, cbg_ref, r0)) * conv(uv_ref, cwv_ref, cbv_ref, r0)).astype(BF16)
        o_ref[r0:r0 + rc, :] += jnp.dot(act, wd_ref[...], preferred_element_type=F32)

    if final:
        @pl.when(j == pl.num_programs(1) - 1)
        def _():
            o_ref[...] = _rms(o_ref[...], fn_ref[...])


def _ffn(x2, ln, w_up, conv_w, conv_b, w_down, fn, *, seq, final):
    T, D = x2.shape
    dff = w_down.shape[0]
    tm = _row_tile(seq, 1024)
    tf = 512
    assert dff % tf == 0 and w_up.shape == (D, 2 * dff)
    nsb = seq // tm
    nf = dff // tf
    return pl.pallas_call(
        functools.partial(_ffn_kernel, nsb=nsb, final=final),
        grid=(T // tm, nf),
        in_specs=[
            pl.BlockSpec((tm, D), lambda i, j: (i, 0)),
            pl.BlockSpec((1, D), lambda i, j: (0, 0)),
            pl.BlockSpec((D, tf), lambda i, j: (0, j)),
            pl.BlockSpec((D, tf), lambda i, j: (0, nf + j)),
            pl.BlockSpec((FFN_CONV, tf), lambda i, j: (0, j)),
            pl.BlockSpec((FFN_CONV, tf), lambda i, j: (0, nf + j)),
            pl.BlockSpec((1, tf), lambda i, j: (0, j)),
            pl.BlockSpec((1, tf), lambda i, j: (0, nf + j)),
            pl.BlockSpec((tf, D), lambda i, j: (j, 0)),
            pl.BlockSpec((1, D), lambda i, j: (0, 0)),
        ],
        out_specs=pl.BlockSpec((tm, D), lambda i, j: (i, 0)),
        out_shape=jax.ShapeDtypeStruct((T, D), F32),
        scratch_shapes=[
            pltpu.VMEM((tm, D), BF16),
            pltpu.VMEM((nf, 2, 8, tf), F32),
            pltpu.VMEM((tm + 8, tf), F32),
            pltpu.VMEM((tm + 8, tf), F32),
        ],
        compiler_params=_cparams("arbitrary", "arbitrary"),
        name="ffn_final" if final else "ffn",
    )(x2, ln, w_up, w_up, conv_w, conv_w, conv_b, conv_b, w_down, fn)


def _rope_tables(seq):
    half = ATTN_HEAD_DIM // 2
    inv_freq = jnp.power(ROPE_THETA, -jnp.arange(half, dtype=F32) / half)
    ang = jnp.arange(seq, dtype=F32)[:, None] * inv_freq[None, :]
    cos, sin = jnp.cos(ang), jnp.sin(ang)
    return jnp.concatenate([cos, cos], axis=-1), jnp.concatenate([-sin, sin], axis=-1)


def _chunk_constants():
    L = SSD_CHUNK
    low = np.tril(np.ones((L, L), np.float32))
    tri = jnp.asarray(np.where(low > 0, 0.0, NEG_INF).astype(np.float32))
    ltri = jnp.asarray(low).astype(BF16)
    return tri, ltri


def _block_onehot(seq):
    oh = np.zeros((seq, LANES), np.float32)
    oh[np.arange(seq), np.arange(seq) // MOBA_BLOCK] = 1.0
    return jnp.asarray(oh).astype(BF16)


def _pad_lanes(v):
    return jnp.pad(v, (0, LANES - v.shape[0]))[None, :]


def kernel(x, ln1, w_in, attn_norm, ssm_conv_w, ssm_conv_b, dt_bias, a_log, d_skip, ssm_norm, w_out, ln2, w_up, ffn_conv_w, ffn_conv_b, w_down, final_norm):
    batch, seq, d_model = x.shape
    depth = ln1.shape[0]
    attn_w = attn_norm.shape[1]
    ssm_w = ssm_norm.shape[1]
    xbc_w = ssm_conv_w.shape[2]
    n_ssm_heads = a_log.shape[1]
    assert ssm_w // n_ssm_heads == SSM_HEAD_DIM and n_ssm_heads <= LANES
    main_w = 3 * attn_w + ssm_w + xbc_w
    cos_t, sin_t = _rope_tables(seq)
    tri, ltri = _chunk_constants()
    onehot = _block_onehot(seq)

    x2 = x.reshape(batch * seq, d_model)
    for i in range(depth):
        w_main = w_in[i].astype(BF16)
        w_dt = jnp.pad(w_main[:, main_w:], ((0, 0), (0, LANES - n_ssm_heads)))
        q, k, v, z2, xbc2, dt2 = _proj_in(
            x2, ln1[i][None, :], w_main, w_dt, cos_t, sin_t, ssm_conv_w[i], ssm_conv_b[i][None, :],
            batch=batch, seq=seq, attn_w=attn_w, ssm_w=ssm_w, xbc_w=xbc_w)

        attn = _moba(q, k, v, onehot)

        y3 = _ssd(
            xbc2.reshape(batch, seq, xbc_w), dt2.reshape(batch, seq, LANES),
            z2.reshape(batch, seq, ssm_w),
            _pad_lanes(dt_bias[i]), _pad_lanes(a_log[i]),
            jnp.repeat(d_skip[i], SSM_HEAD_DIM)[None, :], ssm_norm[i][None, :], tri, ltri,
            ssm_w=ssm_w)

        x2 = _proj_out(attn, y3.reshape(batch * seq, ssm_w), attn_norm[i][None, :],
                       w_out[i].astype(BF16), x2, seq=seq)

        x2 = _ffn(x2, ln2[i][None, :], w_up[i].astype(BF16), ffn_conv_w[i], ffn_conv_b[i][None, :],
                  w_down[i].astype(BF16), final_norm[None, :], seq=seq, final=(i == depth - 1))
    return x2.reshape(batch, seq, d_model)
```
